```python
import math
import jax, jax.numpy as jnp
from jax import lax
import numpy as np

D_MODEL = 2048
BATCH = 1
SEQ = 8192
DEPTH = 1
DEC_BATCH = 32
DEC_SEQ = 1
PAST_LEN = 16384
PAGE_SIZE = 128

N_META = 16
N_HEADS = 8
D_HEAD = 128
D_ATTN = N_HEADS * D_HEAD
D_SSM = D_MODEL // 2
SSM_GROUP = 16
N_GROUPS = D_SSM // SSM_GROUP
P_STATE = 64
D_FF = -(-(8 * D_MODEL) // (3 * 256)) * 256
D_IN_PROJ = 3 * D_ATTN + N_HEADS + D_SSM + 2 * D_MODEL
Q_BLOCK = 128
EPS = 1e-6
DT_MIN = 1e-3
DT_MAX = 1e-1

kernel_name = 'hybrid_fox_s5_step'

F32 = jnp.float32


def rmsnorm(x, g):
    xf = x.astype(F32)
    y = xf * lax.rsqrt(jnp.mean(xf * xf, axis=-1, keepdims=True) + EPS)
    return (y * g.astype(F32)).astype(x.dtype)


def project(h, w_in, b_f, q_g, k_g):
    z = h @ w_in
    cuts = [D_ATTN, 2 * D_ATTN, 3 * D_ATTN, 3 * D_ATTN + N_HEADS,
            3 * D_ATTN + N_HEADS + D_SSM, 3 * D_ATTN + N_HEADS + D_SSM + D_MODEL]
    q, k, v, f, u, ga, gs = jnp.split(z, cuts, axis=-1)
    lead = h.shape[:-1]
    q = rmsnorm(q.reshape(*lead, N_HEADS, D_HEAD), q_g)
    k = rmsnorm(k.reshape(*lead, N_HEADS, D_HEAD), k_g)
    v = v.reshape(*lead, N_HEADS, D_HEAD)
    logf = jax.nn.log_sigmoid(f.astype(F32) + b_f.astype(F32))
    return q, k, v, logf, u, ga, gs


def fox_attend(q, k, v, cq, ck, qpos, kpos):
    s = jnp.einsum('bqhd,bkhd->bhqk', q, k) * (D_HEAD ** -0.5)
    s = s + jnp.swapaxes(cq, 1, 2)[..., :, None] - jnp.swapaxes(ck, 1, 2)[..., None, :]
    causal = kpos[None, :] <= qpos[:, None]
    s = jnp.where(causal[None, None], s, -jnp.inf)
    p = jax.nn.softmax(s, axis=-1)
    return jnp.einsum('bhqk,bkhd->bqhd', p, v)


def fox_prompt(q, k, v, logf):
    B, L = q.shape[:2]
    q = q.astype(F32); k = k.astype(F32); v = v.astype(F32)
    c = jnp.cumsum(logf, axis=1)
    pos = jnp.arange(L)
    o_meta = fox_attend(q[:, :N_META], k[:, :N_META], v[:, :N_META],
                        c[:, :N_META], c[:, :N_META], pos[:N_META], pos[:N_META])
    nb = (L - N_META) // Q_BLOCK
    qb = q[:, N_META:].reshape(B, nb, Q_BLOCK, N_HEADS, D_HEAD).transpose(1, 0, 2, 3, 4)
    cb = c[:, N_META:].reshape(B, nb, Q_BLOCK, N_HEADS).transpose(1, 0, 2, 3)
    pb = pos[N_META:].reshape(nb, Q_BLOCK)
    o = lax.map(lambda a: fox_attend(a[0], k, v, a[1], c, a[2], pos), (qb, cb, pb))
    o = o.transpose(1, 0, 2, 3, 4).reshape(B, L - N_META, N_HEADS, D_HEAD)
    return jnp.concatenate([o_meta, o], axis=1).reshape(B, L, D_ATTN)


def fox_sample(q, k, v, logf, pool_k, pool_v, pool_logf, page_table):
    Bd, T = q.shape[:2]
    past = page_table.shape[1] * PAGE_SIZE
    k_all = jnp.concatenate([pool_k[page_table].reshape(Bd, past, N_HEADS, D_HEAD).astype(F32),
                             k.astype(F32)], axis=1)
    v_all = jnp.concatenate([pool_v[page_table].reshape(Bd, past, N_HEADS, D_HEAD).astype(F32),
                             v.astype(F32)], axis=1)
    lf_all = jnp.concatenate([pool_logf[page_table].reshape(Bd, past, N_HEADS).astype(F32),
                              logf], axis=1)
    c = jnp.cumsum(lf_all, axis=1)
    kpos = jnp.arange(past + T)
    qpos = past + jnp.arange(T)
    o = fox_attend(q.astype(F32), k_all, v_all, c[:, past:], c, qpos, kpos)
    return o.reshape(Bd, T, D_ATTN)


def ssm_combine(left, right):
    a_l, b_l = left
    a_r, b_r = right
    return a_r * a_l, a_r * b_l + b_r


def ssm_branch(u, h0, lam_re, lam_im, log_dt, b_re, b_im, c_re, c_im, d_skip, w_glu, b_glu):
    Bn, L = u.shape[:2]
    lam = lax.complex(lam_re.astype(F32), lam_im.astype(F32))
    dt = jnp.exp(log_dt.astype(F32))[:, None]
    abar = jnp.exp(lam * dt)
    bmat = lax.complex(b_re.astype(F32), b_im.astype(F32))
    bbar = ((abar - 1.0) / lam)[..., None] * bmat
    cmat = lax.complex(c_re.astype(F32), c_im.astype(F32))
    ug = u.reshape(Bn, L, N_GROUPS, SSM_GROUP).astype(F32)
    bu = jnp.einsum('gpc,blgc->blgp', bbar, ug.astype(jnp.complex64))
    a = jnp.concatenate([jnp.ones((Bn, 1, N_GROUPS, P_STATE), jnp.complex64),
                         jnp.broadcast_to(abar, (Bn, L, N_GROUPS, P_STATE))], axis=1)
    b = jnp.concatenate([h0[:, None], bu], axis=1)
    _, xs = lax.associative_scan(ssm_combine, (a, b), axis=1)
    xs = xs[:, 1:]
    y = jnp.einsum('gcp,blgp->blgc', cmat, xs).real + d_skip.astype(F32) * ug
    y = jax.nn.gelu(y.reshape(Bn, L, D_SSM))
    out = y * jax.nn.sigmoid(y @ w_glu.astype(F32) + b_glu.astype(F32))
    return out.astype(u.dtype), xs[:, -1]


def setup_inputs(seed: int = 0) -> dict:
    key = jax.random.key(seed)
    ks = jax.random.split(key, 32)
    n_pages = PAST_LEN // PAGE_SIZE
    n_used = DEC_BATCH * n_pages
    n_pool = n_used + n_used // 4
    nrm = jax.random.normal
    x_prompt = nrm(ks[0], (BATCH, SEQ, D_MODEL), F32)
    x_sample = nrm(ks[1], (DEC_BATCH, DEC_SEQ, D_MODEL), F32)
    cache_k = nrm(ks[2], (DEPTH, n_pool, PAGE_SIZE, N_HEADS, D_HEAD), F32)
    cache_v = nrm(ks[3], (DEPTH, n_pool, PAGE_SIZE, N_HEADS, D_HEAD), F32)
    cache_logf = jax.nn.log_sigmoid(jax.random.uniform(ks[4], (DEPTH, n_pool, PAGE_SIZE, N_HEADS), F32, 2.0, 6.0)
                                    + 0.5 * nrm(ks[5], (DEPTH, n_pool, PAGE_SIZE, N_HEADS), F32))
    state_ssm = 0.5 * nrm(ks[6], (DEPTH, DEC_BATCH, N_GROUPS, P_STATE, 2), F32)
    page_table = jax.random.permutation(ks[7], n_pool)[:n_used].reshape(DEC_BATCH, n_pages).astype(jnp.int32)
    meta = nrm(ks[8], (N_META, D_MODEL), F32)
    norm1_g = 1.0 + 0.02 * nrm(ks[9], (DEPTH, D_MODEL), F32)
    w_in = nrm(ks[10], (DEPTH, D_MODEL, D_IN_PROJ), F32) * D_MODEL ** -0.5
    b_f = jax.random.uniform(ks[11], (DEPTH, N_HEADS), F32, 2.0, 6.0)
    q_norm_g = 1.0 + 0.02 * nrm(ks[12], (DEPTH, D_HEAD), F32)
    k_norm_g = 1.0 + 0.02 * nrm(ks[13], (DEPTH, D_HEAD), F32)
    n_idx = jnp.arange(P_STATE, dtype=F32)
    lam_re = -0.5 + 0.01 * nrm(ks[14], (DEPTH, N_GROUPS, P_STATE), F32)
    lam_im = math.pi * n_idx + 0.01 * nrm(ks[15], (DEPTH, N_GROUPS, P_STATE), F32)
    log_dt = jax.random.uniform(ks[16], (DEPTH, N_GROUPS), F32, math.log(DT_MIN), math.log(DT_MAX))
    b_re = nrm(ks[17], (DEPTH, N_GROUPS, P_STATE, SSM_GROUP), F32) * (2 * SSM_GROUP) ** -0.5
    b_im = nrm(ks[18], (DEPTH, N_GROUPS, P_STATE, SSM_GROUP), F32) * (2 * SSM_GROUP) ** -0.5
    c_re = nrm(ks[19], (DEPTH, N_GROUPS, SSM_GROUP, P_STATE), F32) * (2 * P_STATE) ** -0.5
    c_im = nrm(ks[20], (DEPTH, N_GROUPS, SSM_GROUP, P_STATE), F32) * (2 * P_STATE) ** -0.5
    d_skip = nrm(ks[21], (DEPTH, N_GROUPS, SSM_GROUP), F32)
    w_glu = nrm(ks[22], (DEPTH, D_SSM, D_SSM), F32) * D_SSM ** -0.5
    b_glu = 0.02 * nrm(ks[23], (DEPTH, D_SSM), F32)
    w_br_attn = nrm(ks[24], (DEPTH, D_ATTN, D_MODEL), F32) * D_ATTN ** -0.5
    w_br_ssm = nrm(ks[25], (DEPTH, D_SSM, D_MODEL), F32) * D_SSM ** -0.5
    w_out = nrm(ks[26], (DEPTH, D_MODEL, D_MODEL), F32) * D_MODEL ** -0.5
    norm2_g = 1.0 + 0.02 * nrm(ks[27], (DEPTH, D_MODEL), F32)
    w_ff1 = nrm(ks[28], (DEPTH, D_MODEL, D_FF), F32) * D_MODEL ** -0.5
    w_ff3 = nrm(ks[29], (DEPTH, D_MODEL, D_FF), F32) * D_MODEL ** -0.5
    w_ff2 = nrm(ks[30], (DEPTH, D_FF, D_MODEL), F32) * D_FF ** -0.5
    return {'x_prompt': x_prompt, 'x_sample': x_sample, 'cache_k': cache_k, 'cache_v': cache_v,
            'cache_logf': cache_logf, 'state_ssm': state_ssm, 'page_table': page_table, 'meta': meta,
            'norm1_g': norm1_g, 'w_in': w_in, 'b_f': b_f, 'q_norm_g': q_norm_g, 'k_norm_g': k_norm_g,
            'lam_re': lam_re, 'lam_im': lam_im, 'log_dt': log_dt, 'b_re': b_re, 'b_im': b_im,
            'c_re': c_re, 'c_im': c_im, 'd_skip': d_skip, 'w_glu': w_glu, 'b_glu': b_glu,
            'w_br_attn': w_br_attn, 'w_br_ssm': w_br_ssm, 'w_out': w_out, 'norm2_g': norm2_g,
            'w_ff1': w_ff1, 'w_ff3': w_ff3, 'w_ff2': w_ff2}


def reference(x_prompt, x_sample, cache_k, cache_v, cache_logf, state_ssm, page_table, meta,
              norm1_g, w_in, b_f, q_norm_g, k_norm_g, lam_re, lam_im, log_dt, b_re, b_im,
              c_re, c_im, d_skip, w_glu, b_glu, w_br_attn, w_br_ssm, w_out, norm2_g,
              w_ff1, w_ff3, w_ff2):
    Bp = x_prompt.shape[0]
    xp = jnp.concatenate([jnp.broadcast_to(meta.astype(x_prompt.dtype)[None], (Bp, N_META, D_MODEL)),
                          x_prompt], axis=1)
    xs = x_sample
    kp_l, vp_l, lfp_l, sp_l, ks_l, vs_l, lfs_l, ss_l = [], [], [], [], [], [], [], []

    def merge_and_ffn(x, attn, ssm, ga, gs, l):
        m = (jax.nn.sigmoid(ga) * (attn.astype(x.dtype) @ w_br_attn[l])
             + jax.nn.sigmoid(gs) * (ssm @ w_br_ssm[l]))
        x = x + m @ w_out[l]
        h2 = rmsnorm(x, norm2_g[l])
        return x + (jax.nn.silu(h2 @ w_ff1[l]) * (h2 @ w_ff3[l])) @ w_ff2[l]

    ssm_params = lambda l: (lam_re[l], lam_im[l], log_dt[l], b_re[l], b_im[l], c_re[l], c_im[l],
                            d_skip[l], w_glu[l], b_glu[l])

    for l in range(DEPTH):
        hp = rmsnorm(xp, norm1_g[l])
        qp, kp, vp, lfp, up, gap, gsp = project(hp, w_in[l], b_f[l], q_norm_g[l], k_norm_g[l])
        attn_p = fox_prompt(qp, kp, vp, lfp)
        h0p = jnp.zeros((Bp, N_GROUPS, P_STATE), jnp.complex64)
        ssm_p, hfin_p = ssm_branch(up, h0p, *ssm_params(l))
        xp = merge_and_ffn(xp, attn_p, ssm_p, gap, gsp, l)
        hs = rmsnorm(xs, norm1_g[l])
        qs, kss, vss, lfs, us, gas, gss = project(hs, w_in[l], b_f[l], q_norm_g[l], k_norm_g[l])
        attn_s = fox_sample(qs, kss, vss, lfs, cache_k[l], cache_v[l], cache_logf[l], page_table)
        h0s = lax.complex(state_ssm[l, ..., 0].astype(F32), state_ssm[l, ..., 1].astype(F32))
        ssm_s, hfin_s = ssm_branch(us, h0s, *ssm_params(l))
        xs = merge_and_ffn(xs, attn_s, ssm_s, gas, gss, l)
        kp_l.append(kp); vp_l.append(vp); lfp_l.append(lfp)
        sp_l.append(jnp.stack([hfin_p.real, hfin_p.imag], axis=-1))
        ks_l.append(kss); vs_l.append(vss); lfs_l.append(lfs)
        ss_l.append(jnp.stack([hfin_s.real, hfin_s.imag], axis=-1))

    y_prompt = xp[:, N_META:]
    y_sample = xs
    return (y_prompt, y_sample, jnp.stack(kp_l), jnp.stack(vp_l), jnp.stack(lfp_l), jnp.stack(sp_l),
            jnp.stack(ks_l), jnp.stack(vs_l), jnp.stack(lfs_l), jnp.stack(ss_l))
```

```python
import functools
import math

import jax
import jax.numpy as jnp
from jax import lax
from jax.experimental import pallas as pl
from jax.experimental.pallas import tpu as pltpu

F32 = jnp.float32
BF16 = jnp.bfloat16

N_HEADS = 8
D_HEAD = 128
D_ATTN = N_HEADS * D_HEAD
SSM_GROUP = 16
P_STATE = 64
EPS = 1e-6
SAMPLE_OFF = 16
GROUPS_PER_BLOCK = 8
VMEM_LIMIT = 56 * 1024 * 1024


def _dot(a, b):
    return jnp.dot(a, b, preferred_element_type=F32)


def _split3(x):
    hi = x.astype(BF16)
    r1 = x - hi.astype(F32)
    mid = r1.astype(BF16)
    lo = (r1 - mid.astype(F32)).astype(BF16)
    return hi, mid, lo


def _log_sigmoid(f):
    return jnp.minimum(f, 0.0) - jnp.log1p(jnp.exp(-jnp.abs(f)))


def _head_rmsnorm(z, g):
    outs = []
    for h in range(N_HEADS):
        zh = z[:, h * D_HEAD:(h + 1) * D_HEAD]
        ms = jnp.mean(zh * zh, axis=-1, keepdims=True)
        outs.append(zh * lax.rsqrt(ms + EPS) * g)
    return jnp.concatenate(outs, axis=-1)


def _inproj_kernel(x_ref, g1_ref, w_ref, wf_ref, bf_ref, qg_ref, kg_ref,
                   q_ref, kf_ref, kb_ref, vf_ref, vb_ref, u_ref, sga_ref, sgs_ref, lf_ref,
                   h_scr):
    j = pl.program_id(1)

    @pl.when(j == 0)
    def _():
        x = x_ref[...]
        ms = jnp.mean(x * x, axis=-1, keepdims=True)
        hb = ((x * lax.rsqrt(ms + EPS)) * g1_ref[...]).astype(BF16)
        h_scr[...] = hb
        lf_ref[...] = _log_sigmoid(_dot(hb, wf_ref[...]) + bf_ref[...])

    z = _dot(h_scr[...], w_ref[...])

    @pl.when(j == 0)
    def _():
        q_ref[...] = (_head_rmsnorm(z, qg_ref[...]) * (D_HEAD ** -0.5)).astype(BF16)

    @pl.when(j == 1)
    def _():
        kn = _head_rmsnorm(z, kg_ref[...])
        kf_ref[...] = kn
        kb_ref[...] = kn.astype(BF16)

    @pl.when(j == 2)
    def _():
        vf_ref[...] = z
        vb_ref[...] = z.astype(BF16)

    @pl.when(j == 3)
    def _():
        u_ref[...] = z

    for jj, ref, half in ((4, sga_ref, 0), (5, sga_ref, 1), (6, sgs_ref, 0), (7, sgs_ref, 1)):
        @pl.when(j == jj)
        def _(ref=ref, half=half):
            ref[:, half * 1024:(half + 1) * 1024] = jax.nn.sigmoid(z).astype(BF16)


def _inproj(x_all, g1, w1, wf, bf, qg, kg, *, tm):
    rows, d = x_all.shape
    n_col = w1.shape[1] // 1024
    row_blk = lambda w: pl.BlockSpec((tm, w), lambda i, j: (i, 0))
    full = lambda a: pl.BlockSpec(a.shape, lambda i, j: (0,) * a.ndim)
    outs = [
        jax.ShapeDtypeStruct((rows, D_ATTN), BF16),
        jax.ShapeDtypeStruct((rows, D_ATTN), F32),
        jax.ShapeDtypeStruct((rows, D_ATTN), BF16),
        jax.ShapeDtypeStruct((rows, D_ATTN), F32),
        jax.ShapeDtypeStruct((rows, D_ATTN), BF16),
        jax.ShapeDtypeStruct((rows, 1024), F32),
        jax.ShapeDtypeStruct((rows, d), BF16),
        jax.ShapeDtypeStruct((rows, d), BF16),
        jax.ShapeDtypeStruct((rows, 128), F32),
    ]
    return pl.pallas_call(
        _inproj_kernel,
        grid=(rows // tm, n_col),
        in_specs=[row_blk(d), full(g1), pl.BlockSpec((d, 1024), lambda i, j: (0, j)),
                  full(wf), full(bf), full(qg), full(kg)],
        out_specs=[row_blk(o.shape[1]) for o in outs],
        out_shape=outs,
        scratch_shapes=[pltpu.VMEM((tm, d), BF16)],
        compiler_params=pltpu.CompilerParams(
            dimension_semantics=("arbitrary", "arbitrary"), vmem_limit_bytes=VMEM_LIMIT),
        name="inproj",
    )(x_all, g1, w1, wf, bf, qg, kg)


def _cumsum_kernel(lf_ref, crow_ref, carry_scr):
    @pl.when(pl.program_id(0) == 0)
    def _():
        carry_scr[...] = jnp.zeros_like(carry_scr)

    lf = lf_ref[...]
    t = lf.shape[0]
    r = lax.broadcasted_iota(jnp.int32, (t, t), 0)
    c = lax.broadcasted_iota(jnp.int32, (t, t), 1)
    tri = (c <= r).astype(BF16)
    hi, mid, lo = _split3(lf)
    cs = _dot(tri, hi) + _dot(tri, mid) + _dot(tri, lo) + carry_scr[...]
    carry_scr[...] = cs[t - 1:t, :]
    crow_ref[...] = cs.T[:N_HEADS, :]


def _cumsum_rows(lf, *, n_rows, t):
    return pl.pallas_call(
        _cumsum_kernel,
        grid=(n_rows // t,),
        in_specs=[pl.BlockSpec((t, 128), lambda i: (i, 0))],
        out_specs=pl.BlockSpec((N_HEADS, t), lambda i: (0, i)),
        out_shape=jax.ShapeDtypeStruct((N_HEADS, n_rows), F32),
        scratch_shapes=[pltpu.VMEM((1, 128), F32)],
        compiler_params=pltpu.CompilerParams(dimension_semantics=("arbitrary",)),
        name="logf_cumsum",
    )(lf)


def _flash_kernel(q_ref, k_ref, v_ref, c_ref, o_ref, m_scr, l_scr, acc_scr, *, tq, tk, nk):
    i = pl.program_id(1)
    q = q_ref[...]
    m_scr[...] = jnp.full_like(m_scr, -jnp.inf)
    l_scr[...] = jnp.zeros_like(l_scr)
    acc_scr[...] = jnp.zeros_like(acc_scr)
    row0 = i * tq

    def tile(j, masked):
        k0 = pl.multiple_of(j * tk, tk)
        ks = k_ref[pl.ds(k0, tk), :]
        vs = v_ref[pl.ds(k0, tk), :]
        s = lax.dot_general(q, ks, (((1,), (1,)), ((), ())), preferred_element_type=F32)
        s = s - c_ref[j]
        if masked:
            row = row0 + lax.broadcasted_iota(jnp.int32, (tq, tk), 0)
            col = k0 + lax.broadcasted_iota(jnp.int32, (tq, tk), 1)
            s = jnp.where(col <= row, s, -jnp.inf)
        m_prev = m_scr[...]
        m_new = jnp.maximum(m_prev, jnp.max(s, axis=-1, keepdims=True))
        alpha = jnp.exp(m_prev - m_new)
        p = jnp.exp(s - m_new)
        l_scr[...] = alpha * l_scr[...] + jnp.sum(p, axis=-1, keepdims=True)
        acc_scr[...] = alpha * acc_scr[...] + _dot(p.astype(BF16), vs)
        m_scr[...] = m_new

    n_all = jnp.minimum((row0 + tq + tk - 1) // tk, nk)
    n_full = jnp.minimum((row0 + 1) // tk, n_all)

    def full_body(j, carry):
        tile(j, False)
        return carry

    def masked_body(j, carry):
        tile(j, True)
        return carry

    lax.fori_loop(0, n_full, full_body, 0)
    lax.fori_loop(n_full, n_all, masked_body, 0)
    o_ref[...] = (acc_scr[...] * (1.0 / l_scr[...])).astype(o_ref.dtype)


def _flash(q, kb, vb, c_tiles, *, tq, tk):
    rows = q.shape[0]
    nk = c_tiles.shape[1]
    kernel = functools.partial(_flash_kernel, tq=tq, tk=tk, nk=nk)
    return pl.pallas_call(
        kernel,
        grid=(N_HEADS, rows // tq),
        in_specs=[pl.BlockSpec((tq, D_HEAD), lambda h, i: (i, h)),
                  pl.BlockSpec((nk * tk, D_HEAD), lambda h, i: (0, h)),
                  pl.BlockSpec((nk * tk, D_HEAD), lambda h, i: (0, h)),
                  pl.BlockSpec((None, nk, 1, tk), lambda h, i: (h, 0, 0, 0))],
        out_specs=pl.BlockSpec((tq, D_HEAD), lambda h, i: (i, h)),
        out_shape=jax.ShapeDtypeStruct((rows, D_ATTN), BF16),
        scratch_shapes=[pltpu.VMEM((tq, 1), F32), pltpu.VMEM((tq, 1), F32),
                        pltpu.VMEM((tq, D_HEAD), F32)],
        compiler_params=pltpu.CompilerParams(
            dimension_semantics=("arbitrary", "arbitrary"), vmem_limit_bytes=VMEM_LIMIT),
        name="fox_prompt",
    )(q, kb, vb, c_tiles)


def _decode_kernel(pt_ref, q_ref, kc_ref, vc_ref, lfc_ref, *refs, pp):
    del pt_ref
    k_refs = refs[:pp]
    v_refs = refs[pp:2 * pp]
    lf_refs = refs[2 * pp:3 * pp]
    o_ref = refs[3 * pp]
    m_scr, l_scr, acc_scr, run_scr, wide_scr = refs[3 * pp + 1:]
    g = pl.program_id(1)
    n_g = pl.num_programs(1)

    q = q_ref[0]
    lane = lax.broadcasted_iota(jnp.int32, (N_HEADS, 128), 1)
    sub = lax.broadcasted_iota(jnp.int32, (N_HEADS, 128), 0)
    diag = lane == sub
    e_r = lax.broadcasted_iota(jnp.int32, (N_HEADS, 128), 0)
    e_c = lax.broadcasted_iota(jnp.int32, (N_HEADS, 128), 1)
    widen = (e_r == e_c).astype(F32)

    @pl.when(g == 0)
    def _():
        m_scr[...] = jnp.full_like(m_scr, -jnp.inf)
        l_scr[...] = jnp.zeros_like(l_scr)
        acc_scr[...] = jnp.zeros_like(acc_scr)
        run_scr[...] = jnp.where(diag, jnp.broadcast_to(lfc_ref[0], (N_HEADS, 128)), 0.0)

    for pi in range(pp):
        hi, mid, lo = _split3(lf_refs[pi][...])
        wide_scr[...] = (_dot(hi.astype(F32), widen) + _dot(mid.astype(F32), widen)
                         + _dot(lo.astype(F32), widen))
        k = k_refs[pi][...]
        v = v_refs[pi][...]
        run = run_scr[...]
        biases = [None] * 128
        for t in range(127, -1, -1):
            biases[t] = run
            row = jnp.broadcast_to(wide_scr[t:t + 1, :], (N_HEADS, 128))
            run = run + jnp.where(diag, row, 0.0)
        run_scr[...] = run
        s = jnp.sum(k * q[None] + jnp.stack(biases, axis=0), axis=-1, keepdims=True)
        m_prev = m_scr[...]
        m_new = jnp.maximum(m_prev, jnp.max(s, axis=0))
        alpha = jnp.exp(m_prev - m_new)
        p = jnp.exp(s - m_new[None])
        l_scr[...] = alpha * l_scr[...] + jnp.sum(p, axis=0)
        acc_scr[...] = alpha * acc_scr[...] + jnp.sum(p * v, axis=0)
        m_scr[...] = m_new

    @pl.when(g == n_g - 1)
    def _():
        s_cur = jnp.sum(kc_ref[0] * q, axis=-1, keepdims=True)
        m_prev = m_scr[...]
        m_new = jnp.maximum(m_prev, s_cur)
        alpha = jnp.exp(m_prev - m_new)
        p = jnp.exp(s_cur - m_new)
        l = alpha * l_scr[...] + p
        acc = alpha * acc_scr[...] + p * vc_ref[0]
        o_ref[0] = acc * (1.0 / l)


def _decode(page_table, q_s, k_cur, v_cur, lf_cur, cache_k, cache_v, cache_lf, *, pp):
    b, n_pages = page_table.shape
    n_g = n_pages // pp
    kernel = functools.partial(_decode_kernel, pp=pp)
    seq_blk = lambda w: pl.BlockSpec((1, w, 128), lambda s, g, pt: (s, 0, 0))

    def page_map(pi, nd):
        def index_map(s, g, pt):
            page = pt[s * n_pages + (n_pages - 1 - (g * pp + pi))]
            return (page,) + (0,) * nd
        return index_map

    k_specs = [pl.BlockSpec((None, 128, N_HEADS, D_HEAD), page_map(pi, 3)) for pi in range(pp)]
    v_specs = [pl.BlockSpec((None, 128, N_HEADS, D_HEAD), page_map(pi, 3)) for pi in range(pp)]
    lf_specs = [pl.BlockSpec((None, 128, N_HEADS), page_map(pi, 2)) for pi in range(pp)]
    grid_spec = pltpu.PrefetchScalarGridSpec(
        num_scalar_prefetch=1,
        grid=(b, n_g),
        in_specs=[seq_blk(N_HEADS), seq_blk(N_HEADS), seq_blk(N_HEADS), seq_blk(1)]
                 + k_specs + v_specs + lf_specs,
        out_specs=pl.BlockSpec((1, N_HEADS, D_HEAD), lambda s, g, pt: (s, 0, 0)),
        scratch_shapes=[pltpu.VMEM((N_HEADS, 1), F32), pltpu.VMEM((N_HEADS, 1), F32),
                        pltpu.VMEM((N_HEADS, D_HEAD), F32), pltpu.VMEM((N_HEADS, 128), F32),
                        pltpu.VMEM((128, 128), F32)],
    )
    return pl.pallas_call(
        kernel,
        grid_spec=grid_spec,
        out_shape=jax.ShapeDtypeStruct((b, N_HEADS, D_HEAD), F32),
        compiler_params=pltpu.CompilerParams(
            dimension_semantics=("arbitrary", "arbitrary"), vmem_limit_bytes=VMEM_LIMIT),
        name="fox_decode",
    )(page_table.reshape(-1), q_s, k_cur, v_cur, lf_cur,
      *([cache_k] * pp), *([cache_v] * pp), *([cache_lf] * pp))


def _gelu_tanh(y):
    return 0.5 * y * (1.0 + jnp.tanh(math.sqrt(2.0 / math.pi) * (y + 0.044715 * (y * y * y))))


def _ssm_kernel(u_ref, perm_ref, permt_ref, bre_ref, bim_ref, cre_ref, cim_ref,
                are_ref, aim_ref, akre_ref, akim_ref, d_ref, wg_ref, bg_ref, h0re_ref, h0im_ref,
                o_ref, hpre_ref, hpim_ref, hsre_ref, hsim_ref,
                sre_scr, sim_scr, endre_scr, endim_scr, hinre_scr, hinim_scr, cre_scr, cim_scr,
                *, n_prompt_chunks, steps, lane_chunk, n_sample):
    c = pl.program_id(0)
    t_rows, n_state = sre_scr.shape
    n_blk = n_state // (GROUPS_PER_BLOCK * P_STATE)
    w_blk = GROUPS_PER_BLOCK * P_STATE

    def input_proj(ub):
        for j in range(n_blk):
            uj = ub[:, j * 128:(j + 1) * 128]
            sre_scr[:, j * w_blk:(j + 1) * w_blk] = _dot(uj, bre_ref[j])
            sim_scr[:, j * w_blk:(j + 1) * w_blk] = _dot(uj, bim_ref[j])

    def tail(uf):
        ys = []
        for j in range(n_blk):
            hr = sre_scr[:, j * w_blk:(j + 1) * w_blk].astype(BF16)
            hi = sim_scr[:, j * w_blk:(j + 1) * w_blk].astype(BF16)
            ys.append(_dot(hr, cre_ref[j]) + _dot(hi, cim_ref[j]))
        y = jnp.concatenate(ys, axis=-1) + d_ref[...] * uf
        gl = _gelu_tanh(y)
        return gl * jax.nn.sigmoid(_dot(gl.astype(BF16), wg_ref[...]) + bg_ref[...])

    def scan_pass(store):
        for lc in range(n_state // lane_chunk):
            sl = slice(lc * lane_chunk, (lc + 1) * lane_chunk)
            ar = jnp.broadcast_to(are_ref[:, sl], (8, lane_chunk))
            ai = jnp.broadcast_to(aim_ref[:, sl], (8, lane_chunk))
            if store:
                init = (hinre_scr[:, sl], hinim_scr[:, sl])
            else:
                init = (jnp.zeros((8, lane_chunk), F32), jnp.zeros((8, lane_chunk), F32))

            def body(k, carry, sl=sl, ar=ar, ai=ai):
                re, im = carry
                r0 = pl.multiple_of(k * 8, 8)
                nre = ar * re - ai * im + sre_scr[pl.ds(r0, 8), sl]
                nim = ar * im + ai * re + sim_scr[pl.ds(r0, 8), sl]
                if store:
                    sre_scr[pl.ds(r0, 8), sl] = nre
                    sim_scr[pl.ds(r0, 8), sl] = nim
                return nre, nim

            re, im = lax.fori_loop(0, steps, body, init)
            if not store:
                endre_scr[:, sl] = re
                endim_scr[:, sl] = im

    @pl.when(c == 0)
    def _():
        cre_scr[...] = jnp.zeros_like(cre_scr)
        cim_scr[...] = jnp.zeros_like(cim_scr)

    @pl.when(c < n_prompt_chunks)
    def _():
        hi, mid, lo = _split3(u_ref[...])
        perm = perm_ref[...]
        up = _dot(perm, hi) + _dot(perm, mid) + _dot(perm, lo)
        input_proj(up.astype(BF16))
        scan_pass(False)
        cr = cre_scr[...]
        ci = cim_scr[...]
        akr = akre_ref[...]
        aki = akim_ref[...]
        for s in range(8):
            hinre_scr[s:s + 1, :] = cr
            hinim_scr[s:s + 1, :] = ci
            er = endre_scr[s:s + 1, :]
            ei = endim_scr[s:s + 1, :]
            cr, ci = er + (akr * cr - aki * ci), ei + (akr * ci + aki * cr)
        cre_scr[...] = cr
        cim_scr[...] = ci
        hpre_ref[...] = cr
        hpim_ref[...] = ci
        scan_pass(True)
        out = tail(up)
        o_ref[...] = _dot(permt_ref[...], out.astype(BF16)).astype(o_ref.dtype)

    @pl.when(c == n_prompt_chunks)
    def _():
        uf = u_ref[...]
        input_proj(uf.astype(BF16))
        rs = slice(SAMPLE_OFF, SAMPLE_OFF + n_sample)
        ar = are_ref[...]
        ai = aim_ref[...]
        h0r = h0re_ref[...]
        h0i = h0im_ref[...]
        nre = ar * h0r - ai * h0i + sre_scr[rs, :]
        nim = ar * h0i + ai * h0r + sim_scr[rs, :]
        sre_scr[rs, :] = nre
        sim_scr[rs, :] = nim
        hsre_ref[...] = nre
        hsim_ref[...] = nim
        o_ref[...] = tail(uf).astype(o_ref.dtype)


def _ssm(u, perm, permt, bre, bim, cre, cim, are, aim, akre, akim, d_skip, w_glu, b_glu, h0re, h0im,
         *, t, n_prompt_chunks, lane_chunk):
    rows = u.shape[0]
    n_state = are.shape[1]
    n_sample = h0re.shape[0]
    full = lambda a: pl.BlockSpec(a.shape, lambda c: (0,) * a.ndim)
    kernel = functools.partial(_ssm_kernel, n_prompt_chunks=n_prompt_chunks, steps=t // 8,
                               lane_chunk=lane_chunk, n_sample=n_sample)
    state_row = jax.ShapeDtypeStruct((1, n_state), F32)
    state_smp = jax.ShapeDtypeStruct((n_sample, n_state), F32)
    consts = (perm, permt, bre, bim, cre, cim, are, aim, akre, akim, d_skip, w_glu, b_glu, h0re, h0im)
    return pl.pallas_call(
        kernel,
        grid=(n_prompt_chunks + 1,),
        in_specs=[pl.BlockSpec((t, u.shape[1]), lambda c: (c, 0))] + [full(a) for a in consts],
        out_specs=[pl.BlockSpec((t, u.shape[1]), lambda c: (c, 0)),
                   full(state_row), full(state_row), full(state_smp), full(state_smp)],
        out_shape=[jax.ShapeDtypeStruct((rows, u.shape[1]), BF16),
                   state_row, state_row, state_smp, state_smp],
        scratch_shapes=[pltpu.VMEM((t, n_state), F32), pltpu.VMEM((t, n_state), F32),
                        pltpu.VMEM((8, n_state), F32), pltpu.VMEM((8, n_state), F32),
                        pltpu.VMEM((8, n_state), F32), pltpu.VMEM((8, n_state), F32),
                        pltpu.VMEM((1, n_state), F32), pltpu.VMEM((1, n_state), F32)],
        compiler_params=pltpu.CompilerParams(
            dimension_semantics=("arbitrary",), vmem_limit_bytes=VMEM_LIMIT),
        name="s5_branch",
    )(u, *consts)


def _merge_kernel(attn_ref, ssm_ref, sga_ref, sgs_ref, x_ref, wba_ref, wbs_ref, wo_ref, g2_ref,
                  x1_ref, h2_ref):
    m = (sga_ref[...].astype(F32) * _dot(attn_ref[...], wba_ref[...])
         + sgs_ref[...].astype(F32) * _dot(ssm_ref[...], wbs_ref[...]))
    x1 = x_ref[...] + _dot(m.astype(BF16), wo_ref[...])
    x1_ref[...] = x1
    ms = jnp.mean(x1 * x1, axis=-1, keepdims=True)
    h2_ref[...] = ((x1 * lax.rsqrt(ms + EPS)) * g2_ref[...]).astype(BF16)


def _merge(attn, ssm, sga, sgs, x_all, wba, wbs, wo, g2, *, tm):
    rows, d = x_all.shape
    row_blk = lambda w: pl.BlockSpec((tm, w), lambda i: (i, 0))
    const = lambda a: pl.BlockSpec(a.shape, lambda i: (0,) * a.ndim, pipeline_mode=pl.Buffered(1))
    return pl.pallas_call(
        _merge_kernel,
        grid=(rows // tm,),
        in_specs=[row_blk(D_ATTN), row_blk(ssm.shape[1]), row_blk(d), row_blk(d), row_blk(d),
                  const(wba), const(wbs), const(wo), const(g2)],
        out_specs=[row_blk(d), row_blk(d)],
        out_shape=[jax.ShapeDtypeStruct((rows, d), F32), jax.ShapeDtypeStruct((rows, d), BF16)],
        compiler_params=pltpu.CompilerParams(
            dimension_semantics=("arbitrary",), vmem_limit_bytes=VMEM_LIMIT),
        name="merge",
    )(attn, ssm, sga, sgs, x_all, wba, wbs, wo, g2)


def _ffn_kernel(h2_ref, x1_ref, w1_ref, w3_ref, w2_ref, o_ref):
    @pl.when(pl.program_id(1) == 0)
    def _():
        o_ref[...] = x1_ref[...]

    h2 = h2_ref[...]
    a = _dot(h2, w1_ref[...])
    b = _dot(h2, w3_ref[...])
    o_ref[...] += _dot(((a * jax.nn.sigmoid(a)) * b).astype(BF16), w2_ref[...])


def _ffn(h2, x1, w1, w3, w2, *, tm, tf):
    rows, d = x1.shape
    d_ff = w1.shape[1]
    return pl.pallas_call(
        _ffn_kernel,
        grid=(rows // tm, d_ff // tf),
        in_specs=[pl.BlockSpec((tm, d), lambda i, f: (i, 0)),
                  pl.BlockSpec((tm, d), lambda i, f: (i, 0)),
                  pl.BlockSpec((d, tf), lambda i, f: (0, f)),
                  pl.BlockSpec((d, tf), lambda i, f: (0, f)),
                  pl.BlockSpec((tf, d), lambda i, f: (f, 0))],
        out_specs=pl.BlockSpec((tm, d), lambda i, f: (i, 0)),
        out_shape=jax.ShapeDtypeStruct((rows, d), F32),
        compiler_params=pltpu.CompilerParams(
            dimension_semantics=("arbitrary", "arbitrary"), vmem_limit_bytes=VMEM_LIMIT),
        name="ffn",
    )(h2, x1, w1, w3, w2)


def _ssm_tables(lam_re, lam_im, log_dt, b_re, b_im, c_re, c_im, steps):
    n_groups = lam_re.shape[0]
    lam = lax.complex(lam_re.astype(F32), lam_im.astype(F32))
    dt = jnp.exp(log_dt.astype(F32))[:, None]
    abar = jnp.exp(lam * dt)
    bbar = ((abar - 1.0) / lam)[..., None] * lax.complex(b_re.astype(F32), b_im.astype(F32))
    abar_k = abar
    for _ in range(steps - 1):
        abar_k = abar_k * abar
    n_blk = n_groups // GROUPS_PER_BLOCK
    eye = jnp.eye(GROUPS_PER_BLOCK, dtype=F32)

    def in_blocks(m):
        m = m.reshape(n_blk, GROUPS_PER_BLOCK, P_STATE, SSM_GROUP)
        return jnp.einsum('jgpc,gh->jgchp', m, eye).reshape(
            n_blk, GROUPS_PER_BLOCK * SSM_GROUP, GROUPS_PER_BLOCK * P_STATE).astype(BF16)

    def out_blocks(m):
        m = m.reshape(n_blk, GROUPS_PER_BLOCK, SSM_GROUP, P_STATE)
        return jnp.einsum('jgcp,gh->jgphc', m, eye).reshape(
            n_blk, GROUPS_PER_BLOCK * P_STATE, GROUPS_PER_BLOCK * SSM_GROUP).astype(BF16)

    flat = lambda a: a.reshape(1, -1).astype(F32)
    return (in_blocks(bbar.real), in_blocks(bbar.imag),
            out_blocks(c_re.astype(F32)), out_blocks(-c_im.astype(F32)),
            flat(abar.real), flat(abar.imag), flat(abar_k.real), flat(abar_k.imag))


def _tiles(l_prompt):
    if l_prompt == 8208:
        return dict(t_ssm=432, tm_in=432, tq=864, tk=640, tm_merge=432, tm_ffn=864, tf=512,
                    lane_chunk=512, pp=4)
    return dict(t_ssm=200, tm_in=200, tq=200, tk=128, tm_merge=200, tm_ffn=300, tf=512,
                lane_chunk=512, pp=2)


def kernel(x_prompt, x_sample, cache_k, cache_v, cache_logf, state_ssm, page_table, meta, norm1_g, w_in, b_f, q_norm_g, k_norm_g, lam_re, lam_im, log_dt, b_re, b_im, c_re, c_im, d_skip, w_glu, b_glu, w_br_attn, w_br_ssm, w_out, norm2_g, w_ff1, w_ff3, w_ff2):
    assert x_prompt.shape[0] == 1 and x_sample.shape[1] == 1 and w_in.shape[0] == 1
    n_meta, d = meta.shape
    seq = x_prompt.shape[1]
    n_sample = x_sample.shape[0]
    l_prompt = n_meta + seq
    cfg = _tiles(l_prompt)
    t_ssm = cfg["t_ssm"]
    assert l_prompt % t_ssm == 0 and SAMPLE_OFF + n_sample <= t_ssm
    rows = l_prompt + t_ssm
    s0 = l_prompt + SAMPLE_OFF
    d_ssm = w_glu.shape[1]
    n_groups = d_ssm // SSM_GROUP

    x_all = jnp.concatenate([
        meta.astype(F32), x_prompt[0], jnp.zeros((SAMPLE_OFF, d), F32), x_sample[:, 0],
        jnp.zeros((rows - s0 - n_sample, d), F32)], axis=0)
    w = w_in[0]
    c0, c1, c2 = 3 * D_ATTN, 3 * D_ATTN + N_HEADS, 3 * D_ATTN + N_HEADS + d_ssm
    w1 = jnp.concatenate([w[:, :c0], w[:, c1:]], axis=1).astype(BF16)
    wf = jnp.pad(w[:, c0:c1], ((0, 0), (0, 128 - N_HEADS))).astype(BF16)
    bf = jnp.pad(b_f[0].astype(F32), (0, 128 - N_HEADS)).reshape(1, 128)
    row = lambda a: a.astype(F32).reshape(1, -1)

    q, kf, kb, vf, vb, u, sga, sgs, lf = _inproj(
        x_all, row(norm1_g[0]), w1, wf, bf, row(q_norm_g[0]), row(k_norm_g[0]), tm=cfg["tm_in"])

    tk = cfg["tk"]
    nk = -(-l_prompt // tk)
    c_rows = _cumsum_rows(lf, n_rows=nk * tk, t=tk)
    attn = _flash(q, kb, vb, c_rows.reshape(N_HEADS, nk, 1, tk), tq=cfg["tq"], tk=tk)

    smp = slice(s0, s0 + n_sample)
    q_s = q[smp].astype(F32).reshape(n_sample, N_HEADS, D_HEAD)
    k_cur = kf[smp].reshape(n_sample, N_HEADS, D_HEAD)
    v_cur = vf[smp].reshape(n_sample, N_HEADS, D_HEAD)
    lf_cur = lf[smp].reshape(n_sample, 1, 128)
    attn_s = _decode(page_table, q_s, k_cur, v_cur, lf_cur, cache_k[0], cache_v[0], cache_logf[0],
                     pp=cfg["pp"])
    attn = lax.dynamic_update_slice(attn, attn_s.reshape(n_sample, D_ATTN).astype(BF16), (s0, 0))

    steps = t_ssm // 8
    bre, bim, cre, cim, are, aim, akre, akim = _ssm_tables(
        lam_re[0], lam_im[0], log_dt[0], b_re[0], b_im[0], c_re[0], c_im[0], steps)
    ridx = jnp.arange(t_ssm)
    src = (ridx % 8) * steps + ridx // 8
    perm = (src[:, None] == ridx[None, :]).astype(BF16)
    h0 = state_ssm[0].astype(F32).reshape(n_sample, n_groups * P_STATE, 2)
    ssm, hpre, hpim, hsre, hsim = _ssm(
        u, perm, perm.T, bre, bim, cre, cim, are, aim, akre, akim,
        row(d_skip[0]), w_glu[0].astype(BF16), row(b_glu[0]), h0[..., 0], h0[..., 1],
        t=t_ssm, n_prompt_chunks=l_prompt // t_ssm, lane_chunk=cfg["lane_chunk"])

    x1, h2 = _merge(attn, ssm, sga, sgs, x_all, w_br_attn[0].astype(BF16), w_br_ssm[0].astype(BF16),
                    w_out[0].astype(BF16), row(norm2_g[0]), tm=cfg["tm_merge"])
    y = _ffn(h2, x1, w_ff1[0].astype(BF16), w_ff3[0].astype(BF16), w_ff2[0].astype(BF16),
             tm=cfg["tm_ffn"], tf=cfg["tf"])

    state = lambda re, im, n: jnp.stack([re, im], axis=-1).reshape(1, n, n_groups, P_STATE, 2)
    return (y[n_meta:l_prompt][None],
            y[smp][:, None],
            kf[:l_prompt].reshape(1, 1, l_prompt, N_HEADS, D_HEAD),
            vf[:l_prompt].reshape(1, 1, l_prompt, N_HEADS, D_HEAD),
            lf[:l_prompt, :N_HEADS].reshape(1, 1, l_prompt, N_HEADS),
            state(hpre, hpim, 1),
            kf[smp].reshape(1, n_sample, 1, N_HEADS, D_HEAD),
            vf[smp].reshape(1, n_sample, 1, N_HEADS, D_HEAD),
            lf[smp, :N_HEADS].reshape(1, n_sample, 1, N_HEADS),
            state(hsre, hsim, n_sample))
```

```python
import functools
import math

import jax
import jax.numpy as jnp
from jax import lax
from jax.experimental import pallas as pl
from jax.experimental.pallas import tpu as pltpu

F32 = jnp.float32
BF16 = jnp.bfloat16

N_HEADS = 8
D_HEAD = 128
D_ATTN = N_HEADS * D_HEAD
SSM_GROUP = 16
P_STATE = 64
EPS = 1e-6
SAMPLE_OFF = 16
GROUPS_PER_BLOCK = 8
VMEM_LIMIT = 56 * 1024 * 1024
LOG2E = math.log2(math.e)


def _dot(a, b):
    return jnp.dot(a, b, preferred_element_type=F32)


def _split3(x):
    hi = x.astype(BF16)
    r1 = x - hi.astype(F32)
    mid = r1.astype(BF16)
    lo = (r1 - mid.astype(F32)).astype(BF16)
    return hi, mid, lo


def _log_sigmoid(f):
    return jnp.minimum(f, 0.0) - jnp.log1p(jnp.exp(-jnp.abs(f)))


def _head_rmsnorm(z, g):
    outs = []
    for h in range(N_HEADS):
        zh = z[:, h * D_HEAD:(h + 1) * D_HEAD]
        ms = jnp.mean(zh * zh, axis=-1, keepdims=True)
        outs.append(zh * lax.rsqrt(ms + EPS) * g)
    return jnp.concatenate(outs, axis=-1)


def _inproj_kernel(x_ref, g1_ref, w_ref, wf_ref, bf_ref, qg_ref, kg_ref,
                   q_ref, kf_ref, ks_ref, kb_ref, vf_ref, vs_ref, vb_ref, u_ref, sga_ref, sgs_ref, lf_ref,
                   h_scr, *, n_prompt_tiles, n_sample):
    i = pl.program_id(0)
    j = pl.program_id(1)
    smp = slice(SAMPLE_OFF, SAMPLE_OFF + n_sample)

    @pl.when(j == 0)
    def _():
        x = x_ref[...]
        ms = jnp.mean(x * x, axis=-1, keepdims=True)
        hb = ((x * lax.rsqrt(ms + EPS)) * g1_ref[...]).astype(BF16)
        h_scr[...] = hb
        lf_ref[...] = _log_sigmoid(_dot(hb, wf_ref[...]) + bf_ref[...])

    z = _dot(h_scr[...], w_ref[...])

    def store_heads(val, full_ref, smp_ref):
        @pl.when(i < n_prompt_tiles)
        def _():
            for h in range(N_HEADS):
                full_ref[:, h, :] = val[:, h * D_HEAD:(h + 1) * D_HEAD]

        @pl.when(i == n_prompt_tiles)
        def _():
            for h in range(N_HEADS):
                smp_ref[:, h, :] = val[smp, h * D_HEAD:(h + 1) * D_HEAD]

    @pl.when(j == 0)
    def _():
        q_ref[...] = (_head_rmsnorm(z, qg_ref[...]) * (D_HEAD ** -0.5 * LOG2E)).astype(BF16)

    @pl.when(j == 1)
    def _():
        kn = _head_rmsnorm(z, kg_ref[...])
        kb_ref[...] = kn.astype(BF16)
        store_heads(kn, kf_ref, ks_ref)

    @pl.when(j == 2)
    def _():
        vb_ref[...] = z.astype(BF16)
        store_heads(z, vf_ref, vs_ref)

    @pl.when(j == 3)
    def _():
        u_ref[...] = z

    for jj, ref, half in ((4, sga_ref, 0), (5, sga_ref, 1), (6, sgs_ref, 0), (7, sgs_ref, 1)):
        @pl.when(j == jj)
        def _(ref=ref, half=half):
            ref[:, half * 1024:(half + 1) * 1024] = jax.nn.sigmoid(z).astype(BF16)


def _inproj(x_all, g1, w1, wf, bf, qg, kg, *, tm, l_prompt, n_sample):
    rows, d = x_all.shape
    n_col = w1.shape[1] // 1024
    n_p = l_prompt // tm
    row_blk = lambda w: pl.BlockSpec((tm, w), lambda i, j: (i, 0))
    full = lambda a: pl.BlockSpec(a.shape, lambda i, j: (0,) * a.ndim)
    heads_p = pl.BlockSpec((tm, N_HEADS, D_HEAD), lambda i, j: (jnp.minimum(i, n_p - 1), 0, 0))
    heads_s = pl.BlockSpec((n_sample, N_HEADS, D_HEAD), lambda i, j: (0, 0, 0))
    f32_p = jax.ShapeDtypeStruct((l_prompt, N_HEADS, D_HEAD), F32)
    f32_s = jax.ShapeDtypeStruct((n_sample, N_HEADS, D_HEAD), F32)
    outs = [
        (jax.ShapeDtypeStruct((rows, D_ATTN), BF16), row_blk(D_ATTN)),
        (f32_p, heads_p), (f32_s, heads_s),
        (jax.ShapeDtypeStruct((rows, D_ATTN), BF16), row_blk(D_ATTN)),
        (f32_p, heads_p), (f32_s, heads_s),
        (jax.ShapeDtypeStruct((rows, D_ATTN), BF16), row_blk(D_ATTN)),
        (jax.ShapeDtypeStruct((rows, 1024), F32), row_blk(1024)),
        (jax.ShapeDtypeStruct((rows, d), BF16), row_blk(d)),
        (jax.ShapeDtypeStruct((rows, d), BF16), row_blk(d)),
        (jax.ShapeDtypeStruct((rows, 128), F32), row_blk(128)),
    ]
    kernel = functools.partial(_inproj_kernel, n_prompt_tiles=n_p, n_sample=n_sample)
    return pl.pallas_call(
        kernel,
        grid=(rows // tm, n_col),
        in_specs=[row_blk(d), full(g1), pl.BlockSpec((d, 1024), lambda i, j: (0, j)),
                  full(wf), full(bf), full(qg), full(kg)],
        out_specs=[o[1] for o in outs],
        out_shape=[o[0] for o in outs],
        scratch_shapes=[pltpu.VMEM((tm, d), BF16)],
        compiler_params=pltpu.CompilerParams(
            dimension_semantics=("arbitrary", "arbitrary"), vmem_limit_bytes=VMEM_LIMIT),
        name="inproj",
    )(x_all, g1, w1, wf, bf, qg, kg)


def _cumsum_kernel(lf_ref, crow_ref, carry_scr):
    @pl.when(pl.program_id(0) == 0)
    def _():
        carry_scr[...] = jnp.zeros_like(carry_scr)

    lf = lf_ref[...]
    t = lf.shape[0]
    r = lax.broadcasted_iota(jnp.int32, (t, t), 0)
    c = lax.broadcasted_iota(jnp.int32, (t, t), 1)
    tri = (c <= r).astype(BF16)
    hi, mid, lo = _split3(lf)
    cs = _dot(tri, hi) + _dot(tri, mid) + _dot(tri, lo) + carry_scr[...]
    carry_scr[...] = cs[t - 1:t, :]
    crow_ref[...] = (cs * LOG2E).T[:N_HEADS, :]


def _cumsum_rows(lf, *, n_rows, t):
    return pl.pallas_call(
        _cumsum_kernel,
        grid=(n_rows // t,),
        in_specs=[pl.BlockSpec((t, 128), lambda i: (i, 0))],
        out_specs=pl.BlockSpec((N_HEADS, t), lambda i: (0, i)),
        out_shape=jax.ShapeDtypeStruct((N_HEADS, n_rows), F32),
        scratch_shapes=[pltpu.VMEM((1, 128), F32)],
        compiler_params=pltpu.CompilerParams(dimension_semantics=("arbitrary",)),
        name="logf_cumsum",
    )(lf)


def _flash_kernel(q_ref, k_ref, v_ref, c_ref, zero_ref, o_ref,
                  s_scr, p_scr, m_scr, l_scr, a_scr, acc_scr, *, t, rc):
    del zero_ref
    i = pl.program_id(1)
    q = q_ref[...]
    m_scr[...] = jnp.full_like(m_scr, -jnp.inf)
    l_scr[...] = jnp.zeros_like(l_scr)
    acc_scr[...] = jnp.zeros_like(acc_scr)

    def tile(j, masked):
        k0 = pl.multiple_of(j * t, t)
        s_scr[...] = lax.dot_general(q, k_ref[pl.ds(k0, t), :], (((1,), (1,)), ((), ())),
                                     preferred_element_type=F32)
        cj = c_ref[j]

        def chunk(r, carry):
            rows = pl.ds(pl.multiple_of(r * rc, rc), rc)
            s = s_scr[rows, :] - cj
            if masked:
                row = r * rc + lax.broadcasted_iota(jnp.int32, (rc, t), 0)
                col = lax.broadcasted_iota(jnp.int32, (rc, t), 1)
                s = jnp.where(col <= row, s, -jnp.inf)
            m_prev = m_scr[rows, :]
            m_new = jnp.maximum(m_prev, jnp.max(s, axis=-1, keepdims=True))
            alpha = jnp.exp2(m_prev - m_new)
            p = jnp.exp2(s - m_new)
            l_scr[rows, :] = alpha * l_scr[rows, :] + jnp.sum(p, axis=-1, keepdims=True)
            m_scr[rows, :] = m_new
            a_scr[rows, :] = alpha
            p_scr[rows, :] = p.astype(BF16)
            return carry

        lax.fori_loop(0, t // rc, chunk, 0, unroll=True)
        acc_scr[...] = a_scr[...] * acc_scr[...] + _dot(p_scr[...], v_ref[pl.ds(k0, t), :])

    def full_body(j, carry):
        tile(j, False)
        return carry

    lax.fori_loop(0, i, full_body, 0)
    tile(i, True)
    o_ref[...] = (acc_scr[...] * (1.0 / l_scr[...])).astype(o_ref.dtype)


def _flash(q, kb, vb, c_tiles, *, t, rc):
    rows = q.shape[0]
    nk = c_tiles.shape[1]
    kernel = functools.partial(_flash_kernel, t=t, rc=rc)
    return pl.pallas_call(
        kernel,
        grid=(N_HEADS, nk),
        in_specs=[pl.BlockSpec((t, D_HEAD), lambda h, i: (i, h)),
                  pl.BlockSpec((nk * t, D_HEAD), lambda h, i: (0, h)),
                  pl.BlockSpec((nk * t, D_HEAD), lambda h, i: (0, h)),
                  pl.BlockSpec((None, nk, 1, t), lambda h, i: (h, 0, 0, 0)),
                  pl.BlockSpec(memory_space=pl.ANY)],
        out_specs=pl.BlockSpec((t, D_HEAD), lambda h, i: (i, h)),
        out_shape=jax.ShapeDtypeStruct((rows, D_ATTN), BF16),
        input_output_aliases={4: 0},
        scratch_shapes=[pltpu.VMEM((t, t), F32), pltpu.VMEM((t, t), BF16),
                        pltpu.VMEM((t, 1), F32), pltpu.VMEM((t, 1), F32), pltpu.VMEM((t, 1), F32),
                        pltpu.VMEM((t, D_HEAD), F32)],
        compiler_params=pltpu.CompilerParams(
            dimension_semantics=("arbitrary", "arbitrary"), vmem_limit_bytes=VMEM_LIMIT),
        name="fox_prompt",
    )(q, kb, vb, c_tiles, jnp.zeros((rows, D_ATTN), BF16))


def _decode_kernel(pt_ref, q_ref, kc_ref, vc_ref, lfc_ref, *refs, pp):
    del pt_ref
    k_refs = refs[:pp]
    v_refs = refs[pp:2 * pp]
    lf_refs = refs[2 * pp:3 * pp]
    o_ref = refs[3 * pp]
    m_scr, l_scr, acc_scr, carry_scr, wide_scr = refs[3 * pp + 1:]
    g = pl.program_id(1)
    n_g = pl.num_programs(1)
    page = lf_refs[0].shape[-1]

    q = q_ref[0]
    lane = lax.broadcasted_iota(jnp.int32, (N_HEADS, 128), 1)
    sub = lax.broadcasted_iota(jnp.int32, (N_HEADS, 128), 0)
    diag = lane == sub
    later = (lax.broadcasted_iota(jnp.int32, (page, page), 0)
             > lax.broadcasted_iota(jnp.int32, (page, page), 1)).astype(F32)

    @pl.when(g == 0)
    def _():
        m_scr[...] = jnp.full_like(m_scr, -jnp.inf)
        l_scr[...] = jnp.zeros_like(l_scr)
        acc_scr[...] = jnp.zeros_like(acc_scr)
        cur = jnp.where(diag, jnp.broadcast_to(lfc_ref[0], (N_HEADS, 128)), 0.0)
        carry_scr[...] = jnp.broadcast_to(jnp.sum(cur, axis=-1, keepdims=True) * LOG2E, (N_HEADS, 128))

    for pi in range(pp):
        x = lf_refs[pi][...] * LOG2E
        hi, mid, lo = _split3(x)
        carry = carry_scr[...]
        bias = (_dot(hi.astype(F32), later) + _dot(mid.astype(F32), later)
                + _dot(lo.astype(F32), later) + carry)
        carry_scr[...] = carry + jnp.sum(x, axis=-1, keepdims=True)
        wide_scr[...] = jnp.concatenate([bias, jnp.zeros((page - N_HEADS, page), F32)], axis=0).T
        k = k_refs[pi][...]
        v = v_refs[pi][...]
        ss = []
        for t in range(page):
            b_t = jnp.where(diag, jnp.broadcast_to(wide_scr[t:t + 1, :], (N_HEADS, 128)), 0.0)
            s_t = jnp.sum(k[t] * q + b_t, axis=-1, keepdims=True)
            ss.append(jnp.broadcast_to(s_t, (N_HEADS, 128)))
        s = jnp.stack(ss, axis=0)
        m_prev = m_scr[...]
        m_new = jnp.maximum(m_prev, jnp.max(s, axis=0))
        alpha = jnp.exp2(m_prev - m_new)
        p = jnp.exp2(s - m_new[None])
        l_scr[...] = alpha * l_scr[...] + jnp.sum(p, axis=0)
        acc_scr[...] = alpha * acc_scr[...] + jnp.sum(p * v, axis=0)
        m_scr[...] = m_new

    @pl.when(g == n_g - 1)
    def _():
        s_cur = jnp.sum(kc_ref[0] * q, axis=-1, keepdims=True)
        m_prev = m_scr[...]
        m_new = jnp.maximum(m_prev, s_cur)
        alpha = jnp.exp2(m_prev - m_new)
        p = jnp.exp2(s_cur - m_new)
        l = alpha * l_scr[...] + p
        acc = alpha * acc_scr[...] + p * vc_ref[0]
        o_ref[0] = acc * (1.0 / l)


def _decode(page_table, q_s, k_cur, v_cur, lf_cur, cache_k, cache_v, cache_lft, *, pp):
    b, n_pages = page_table.shape
    page = cache_lft.shape[-1]
    n_g = n_pages // pp
    kernel = functools.partial(_decode_kernel, pp=pp)
    seq_blk = lambda w: pl.BlockSpec((1, w, 128), lambda s, g, pt: (s, 0, 0))

    def page_map(pi, nd):
        def index_map(s, g, pt):
            return (pt[s * n_pages + (n_pages - 1 - (g * pp + pi))],) + (0,) * nd
        return index_map

    k_specs = [pl.BlockSpec((None, page, N_HEADS, D_HEAD), page_map(pi, 3)) for pi in range(pp)]
    v_specs = [pl.BlockSpec((None, page, N_HEADS, D_HEAD), page_map(pi, 3)) for pi in range(pp)]
    lf_specs = [pl.BlockSpec((None, N_HEADS, page), page_map(pi, 2)) for pi in range(pp)]
    grid_spec = pltpu.PrefetchScalarGridSpec(
        num_scalar_prefetch=1,
        grid=(b, n_g),
        in_specs=[seq_blk(N_HEADS), seq_blk(N_HEADS), seq_blk(N_HEADS), seq_blk(1)]
                 + k_specs + v_specs + lf_specs,
        out_specs=pl.BlockSpec((1, N_HEADS, D_HEAD), lambda s, g, pt: (s, 0, 0)),
        scratch_shapes=[pltpu.VMEM((N_HEADS, 128), F32), pltpu.VMEM((N_HEADS, 128), F32),
                        pltpu.VMEM((N_HEADS, D_HEAD), F32), pltpu.VMEM((N_HEADS, 128), F32),
                        pltpu.VMEM((page, page), F32)],
    )
    return pl.pallas_call(
        kernel,
        grid_spec=grid_spec,
        out_shape=jax.ShapeDtypeStruct((b, N_HEADS, D_HEAD), F32),
        compiler_params=pltpu.CompilerParams(
            dimension_semantics=("arbitrary", "arbitrary"), vmem_limit_bytes=VMEM_LIMIT),
        name="fox_decode",
    )(page_table.reshape(-1), q_s, k_cur, v_cur, lf_cur,
      *([cache_k] * pp), *([cache_v] * pp), *([cache_lft] * pp))


def _gelu_tanh(y):
    return 0.5 * y * (1.0 + jnp.tanh(math.sqrt(2.0 / math.pi) * (y + 0.044715 * (y * y * y))))


def _ssm_kernel(u_ref, perm_ref, permt_ref, bre_ref, bim_ref, cre_ref, cim_ref,
                are_ref, aim_ref, akre_ref, akim_ref, d_ref, wg_ref, bg_ref, h0re_ref, h0im_ref,
                o_ref, hpre_ref, hpim_ref, hsre_ref, hsim_ref,
                sre_scr, sim_scr, endre_scr, endim_scr, hinre_scr, hinim_scr, cre_scr, cim_scr,
                *, n_prompt_chunks, steps, lane_chunk, n_sample):
    c = pl.program_id(0)
    t_rows, n_state = sre_scr.shape
    n_blk = n_state // (GROUPS_PER_BLOCK * P_STATE)
    w_blk = GROUPS_PER_BLOCK * P_STATE

    def input_proj(ub):
        for j in range(n_blk):
            uj = ub[:, j * 128:(j + 1) * 128]
            sre_scr[:, j * w_blk:(j + 1) * w_blk] = _dot(uj, bre_ref[j])
            sim_scr[:, j * w_blk:(j + 1) * w_blk] = _dot(uj, bim_ref[j])

    def tail(uf):
        ys = []
        for j in range(n_blk):
            hr = sre_scr[:, j * w_blk:(j + 1) * w_blk].astype(BF16)
            hi = sim_scr[:, j * w_blk:(j + 1) * w_blk].astype(BF16)
            ys.append(_dot(hr, cre_ref[j]) + _dot(hi, cim_ref[j]))
        y = jnp.concatenate(ys, axis=-1) + d_ref[...] * uf
        gl = _gelu_tanh(y)
        return gl * jax.nn.sigmoid(_dot(gl.astype(BF16), wg_ref[...]) + bg_ref[...])

    def scan_pass(store):
        for lc in range(n_state // lane_chunk):
            sl = slice(lc * lane_chunk, (lc + 1) * lane_chunk)
            ar = jnp.broadcast_to(are_ref[:, sl], (8, lane_chunk))
            ai = jnp.broadcast_to(aim_ref[:, sl], (8, lane_chunk))
            if store:
                init = (hinre_scr[:, sl], hinim_scr[:, sl])
            else:
                init = (jnp.zeros((8, lane_chunk), F32), jnp.zeros((8, lane_chunk), F32))

            def body(k, carry, sl=sl, ar=ar, ai=ai):
                re, im = carry
                r0 = pl.multiple_of(k * 8, 8)
                nre = ar * re - ai * im + sre_scr[pl.ds(r0, 8), sl]
                nim = ar * im + ai * re + sim_scr[pl.ds(r0, 8), sl]
                if store:
                    sre_scr[pl.ds(r0, 8), sl] = nre
                    sim_scr[pl.ds(r0, 8), sl] = nim
                return nre, nim

            re, im = lax.fori_loop(0, steps, body, init)
            if not store:
                endre_scr[:, sl] = re
                endim_scr[:, sl] = im

    @pl.when(c == 0)
    def _():
        cre_scr[...] = jnp.zeros_like(cre_scr)
        cim_scr[...] = jnp.zeros_like(cim_scr)

    @pl.when(c < n_prompt_chunks)
    def _():
        hi, mid, lo = _split3(u_ref[...])
        perm = perm_ref[...]
        up = _dot(perm, hi) + _dot(perm, mid) + _dot(perm, lo)
        input_proj(up.astype(BF16))
        scan_pass(False)
        cr = cre_scr[...]
        ci = cim_scr[...]
        akr = akre_ref[...]
        aki = akim_ref[...]
        for s in range(8):
            hinre_scr[s:s + 1, :] = cr
            hinim_scr[s:s + 1, :] = ci
            er = endre_scr[s:s + 1, :]
            ei = endim_scr[s:s + 1, :]
            cr, ci = er + (akr * cr - aki * ci), ei + (akr * ci + aki * cr)
        cre_scr[...] = cr
        cim_scr[...] = ci
        hpre_ref[...] = cr
        hpim_ref[...] = ci
        scan_pass(True)
        out = tail(up)
        o_ref[...] = _dot(permt_ref[...], out.astype(BF16)).astype(o_ref.dtype)

    @pl.when(c == n_prompt_chunks)
    def _():
        uf = u_ref[...]
        input_proj(uf.astype(BF16))
        rs = slice(SAMPLE_OFF, SAMPLE_OFF + n_sample)
        ar = are_ref[...]
        ai = aim_ref[...]
        h0r = h0re_ref[...]
        h0i = h0im_ref[...]
        nre = ar * h0r - ai * h0i + sre_scr[rs, :]
        nim = ar * h0i + ai * h0r + sim_scr[rs, :]
        sre_scr[rs, :] = nre
        sim_scr[rs, :] = nim
        hsre_ref[...] = nre
        hsim_ref[...] = nim
        o_ref[...] = tail(uf).astype(o_ref.dtype)


def _ssm(u, perm, permt, bre, bim, cre, cim, are, aim, akre, akim, d_skip, w_glu, b_glu, h0re, h0im,
         *, t, n_prompt_chunks, lane_chunk):
    rows = u.shape[0]
    n_state = are.shape[1]
    n_sample = h0re.shape[0]
    full = lambda a: pl.BlockSpec(a.shape, lambda c: (0,) * a.ndim)
    kernel = functools.partial(_ssm_kernel, n_prompt_chunks=n_prompt_chunks, steps=t // 8,
                               lane_chunk=lane_chunk, n_sample=n_sample)
    state_row = jax.ShapeDtypeStruct((1, n_state), F32)
    state_smp = jax.ShapeDtypeStruct((n_sample, n_state), F32)
    consts = (perm, permt, bre, bim, cre, cim, are, aim, akre, akim, d_skip, w_glu, b_glu, h0re, h0im)
    return pl.pallas_call(
        kernel,
        grid=(n_prompt_chunks + 1,),
        in_specs=[pl.BlockSpec((t, u.shape[1]), lambda c: (c, 0))] + [full(a) for a in consts],
        out_specs=[pl.BlockSpec((t, u.shape[1]), lambda c: (c, 0)),
                   full(state_row), full(state_row), full(state_smp), full(state_smp)],
        out_shape=[jax.ShapeDtypeStruct((rows, u.shape[1]), BF16),
                   state_row, state_row, state_smp, state_smp],
        scratch_shapes=[pltpu.VMEM((t, n_state), F32), pltpu.VMEM((t, n_state), F32),
                        pltpu.VMEM((8, n_state), F32), pltpu.VMEM((8, n_state), F32),
                        pltpu.VMEM((8, n_state), F32), pltpu.VMEM((8, n_state), F32),
                        pltpu.VMEM((1, n_state), F32), pltpu.VMEM((1, n_state), F32)],
        compiler_params=pltpu.CompilerParams(
            dimension_semantics=("arbitrary",), vmem_limit_bytes=VMEM_LIMIT),
        name="s5_branch",
    )(u, *consts)


def _merge_kernel(attn_ref, as_ref, ssm_ref, sga_ref, sgs_ref, x_ref, wba_ref, wbs_ref, wo_ref, g2_ref,
                  x1_ref, h2_ref, ma_scr, *, sample_tile):
    n_sample = as_ref.shape[0]
    ma_scr[...] = _dot(attn_ref[...], wba_ref[...])

    @pl.when(pl.program_id(0) == sample_tile)
    def _():
        acc = jnp.zeros((n_sample, wba_ref.shape[1]), F32)
        for h in range(N_HEADS):
            acc = acc + _dot(as_ref[:, h, :].astype(BF16), wba_ref[h * D_HEAD:(h + 1) * D_HEAD, :])
        ma_scr[SAMPLE_OFF:SAMPLE_OFF + n_sample, :] = acc

    m = (sga_ref[...].astype(F32) * ma_scr[...]
         + sgs_ref[...].astype(F32) * _dot(ssm_ref[...], wbs_ref[...]))
    x1 = x_ref[...] + _dot(m.astype(BF16), wo_ref[...])
    x1_ref[...] = x1
    ms = jnp.mean(x1 * x1, axis=-1, keepdims=True)
    h2_ref[...] = ((x1 * lax.rsqrt(ms + EPS)) * g2_ref[...]).astype(BF16)


def _merge(attn, attn_s, ssm, sga, sgs, x_all, wba, wbs, wo, g2, *, tm, sample_tile):
    rows, d = x_all.shape
    row_blk = lambda w: pl.BlockSpec((tm, w), lambda i: (i, 0))
    const = lambda a: pl.BlockSpec(a.shape, lambda i: (0,) * a.ndim, pipeline_mode=pl.Buffered(1))
    kernel = functools.partial(_merge_kernel, sample_tile=sample_tile)
    return pl.pallas_call(
        kernel,
        grid=(rows // tm,),
        in_specs=[row_blk(D_ATTN), const(attn_s), row_blk(ssm.shape[1]), row_blk(d), row_blk(d), row_blk(d),
                  const(wba), const(wbs), const(wo), const(g2)],
        out_specs=[row_blk(d), row_blk(d)],
        out_shape=[jax.ShapeDtypeStruct((rows, d), F32), jax.ShapeDtypeStruct((rows, d), BF16)],
        scratch_shapes=[pltpu.VMEM((tm, d), F32)],
        compiler_params=pltpu.CompilerParams(
            dimension_semantics=("arbitrary",), vmem_limit_bytes=VMEM_LIMIT),
        name="merge",
    )(attn, attn_s, ssm, sga, sgs, x_all, wba, wbs, wo, g2)


def _ffn_kernel(h2_ref, x1_ref, w1_ref, w3_ref, w2_ref, o_ref):
    @pl.when(pl.program_id(1) == 0)
    def _():
        o_ref[...] = x1_ref[...]

    h2 = h2_ref[...]
    a = _dot(h2, w1_ref[...])
    b = _dot(h2, w3_ref[...])
    o_ref[...] += _dot(((a * jax.nn.sigmoid(a)) * b).astype(BF16), w2_ref[...])


def _ffn(h2, x1, w1, w3, w2, *, tm, tf):
    rows, d = x1.shape
    d_ff = w1.shape[1]
    return pl.pallas_call(
        _ffn_kernel,
        grid=(rows // tm, d_ff // tf),
        in_specs=[pl.BlockSpec((tm, d), lambda i, f: (i, 0)),
                  pl.BlockSpec((tm, d), lambda i, f: (i, 0)),
                  pl.BlockSpec((d, tf), lambda i, f: (0, f)),
                  pl.BlockSpec((d, tf), lambda i, f: (0, f)),
                  pl.BlockSpec((tf, d), lambda i, f: (f, 0))],
        out_specs=pl.BlockSpec((tm, d), lambda i, f: (i, 0)),
        out_shape=jax.ShapeDtypeStruct((rows, d), F32),
        compiler_params=pltpu.CompilerParams(
            dimension_semantics=("arbitrary", "arbitrary"), vmem_limit_bytes=VMEM_LIMIT),
        name="ffn",
    )(h2, x1, w1, w3, w2)


def _ssm_tables(lam_re, lam_im, log_dt, b_re, b_im, c_re, c_im, steps):
    n_groups = lam_re.shape[0]
    lr, li = lam_re.astype(F32), lam_im.astype(F32)
    dt = jnp.exp(log_dt.astype(F32))[:, None]
    mag = jnp.exp(lr * dt)
    ar, ai = mag * jnp.cos(li * dt), mag * jnp.sin(li * dt)
    den = lr * lr + li * li
    cr = ((ar - 1.0) * lr + ai * li) / den
    ci = (ai * lr - (ar - 1.0) * li) / den
    br, bi = b_re.astype(F32), b_im.astype(F32)
    bbr = cr[..., None] * br - ci[..., None] * bi
    bbi = cr[..., None] * bi + ci[..., None] * br
    akr, aki = ar, ai
    for _ in range(steps - 1):
        akr, aki = akr * ar - aki * ai, akr * ai + aki * ar
    n_blk = n_groups // GROUPS_PER_BLOCK
    eye = jnp.eye(GROUPS_PER_BLOCK, dtype=F32)

    def in_blocks(m):
        m = m.reshape(n_blk, GROUPS_PER_BLOCK, P_STATE, SSM_GROUP)
        return jnp.einsum('jgpc,gh->jgchp', m, eye).reshape(
            n_blk, GROUPS_PER_BLOCK * SSM_GROUP, GROUPS_PER_BLOCK * P_STATE).astype(BF16)

    def out_blocks(m):
        m = m.reshape(n_blk, GROUPS_PER_BLOCK, SSM_GROUP, P_STATE)
        return jnp.einsum('jgcp,gh->jgphc', m, eye).reshape(
            n_blk, GROUPS_PER_BLOCK * P_STATE, GROUPS_PER_BLOCK * SSM_GROUP).astype(BF16)

    flat = lambda a: a.reshape(1, -1)
    return (in_blocks(bbr), in_blocks(bbi),
            out_blocks(c_re.astype(F32)), out_blocks(-c_im.astype(F32)),
            flat(ar), flat(ai), flat(akr), flat(aki))


def _tiles(l_prompt):
    if l_prompt == 8208:
        return dict(t_ssm=432, t_attn=640, rc=32, tm_ffn=864, tf=512, lane_chunk=512, pp=8)
    return dict(t_ssm=200, t_attn=128, rc=32, tm_ffn=300, tf=512, lane_chunk=512, pp=2)


def kernel(x_prompt, x_sample, cache_k, cache_v, cache_logf, state_ssm, page_table, meta, norm1_g, w_in, b_f, q_norm_g, k_norm_g, lam_re, lam_im, log_dt, b_re, b_im, c_re, c_im, d_skip, w_glu, b_glu, w_br_attn, w_br_ssm, w_out, norm2_g, w_ff1, w_ff3, w_ff2):
    assert x_prompt.shape[0] == 1 and x_sample.shape[1] == 1 and w_in.shape[0] == 1
    n_meta, d = meta.shape
    seq = x_prompt.shape[1]
    n_sample = x_sample.shape[0]
    l_prompt = n_meta + seq
    cfg = _tiles(l_prompt)
    t_ssm = cfg["t_ssm"]
    assert l_prompt % t_ssm == 0 and SAMPLE_OFF + n_sample <= t_ssm
    n_p = l_prompt // t_ssm
    rows = l_prompt + t_ssm
    s0 = l_prompt + SAMPLE_OFF
    d_ssm = w_glu.shape[1]
    n_groups = d_ssm // SSM_GROUP

    x_all = jnp.concatenate([
        meta.astype(F32), x_prompt[0], jnp.zeros((SAMPLE_OFF, d), F32), x_sample[:, 0],
        jnp.zeros((rows - s0 - n_sample, d), F32)], axis=0)
    w = w_in[0]
    c0, c1 = 3 * D_ATTN, 3 * D_ATTN + N_HEADS
    w1 = jnp.concatenate([w[:, :c0], w[:, c1:]], axis=1).astype(BF16)
    wf = jnp.pad(w[:, c0:c1], ((0, 0), (0, 128 - N_HEADS))).astype(BF16)
    bf = jnp.pad(b_f[0].astype(F32), (0, 128 - N_HEADS)).reshape(1, 128)
    row = lambda a: a.astype(F32).reshape(1, -1)

    q, kf, ks, kb, vf, vs, vb, u, sga, sgs, lf = _inproj(
        x_all, row(norm1_g[0]), w1, wf, bf, row(q_norm_g[0]), row(k_norm_g[0]),
        tm=t_ssm, l_prompt=l_prompt, n_sample=n_sample)

    ta = cfg["t_attn"]
    nk = -(-l_prompt // ta)
    assert nk * ta <= rows
    c_rows = _cumsum_rows(lf, n_rows=nk * ta, t=ta)
    attn = _flash(q, kb, vb, c_rows.reshape(N_HEADS, nk, 1, ta), t=ta, rc=cfg["rc"])

    smp = slice(s0, s0 + n_sample)
    q_s = q[smp].astype(F32).reshape(n_sample, N_HEADS, D_HEAD)
    lf_cur = lf[smp].reshape(n_sample, 1, 128)
    attn_s = _decode(page_table, q_s, ks, vs, lf_cur, cache_k[0], cache_v[0],
                     jnp.swapaxes(cache_logf[0], 1, 2), pp=cfg["pp"])

    steps = t_ssm // 8
    bre, bim, cre, cim, are, aim, akre, akim = _ssm_tables(
        lam_re[0], lam_im[0], log_dt[0], b_re[0], b_im[0], c_re[0], c_im[0], steps)
    ridx = jnp.arange(t_ssm)
    src = (ridx % 8) * steps + ridx // 8
    perm = (src[:, None] == ridx[None, :]).astype(BF16)
    h0 = state_ssm[0].astype(F32).reshape(n_sample, n_groups * P_STATE, 2)
    ssm, hpre, hpim, hsre, hsim = _ssm(
        u, perm, perm.T, bre, bim, cre, cim, are, aim, akre, akim,
        row(d_skip[0]), w_glu[0].astype(BF16), row(b_glu[0]), h0[..., 0], h0[..., 1],
        t=t_ssm, n_prompt_chunks=n_p, lane_chunk=cfg["lane_chunk"])

    x1, h2 = _merge(attn, attn_s, ssm, sga, sgs, x_all, w_br_attn[0].astype(BF16),
                    w_br_ssm[0].astype(BF16), w_out[0].astype(BF16), row(norm2_g[0]),
                    tm=t_ssm, sample_tile=n_p)
    y = _ffn(h2, x1, w_ff1[0].astype(BF16), w_ff3[0].astype(BF16), w_ff2[0].astype(BF16),
             tm=cfg["tm_ffn"], tf=cfg["tf"])

    state = lambda re, im, n: jnp.stack([re, im], axis=-1).reshape(1, n, n_groups, P_STATE, 2)
    return (y[n_meta:l_prompt][None],
            y[smp][:, None],
            kf.reshape(1, 1, l_prompt, N_HEADS, D_HEAD),
            vf.reshape(1, 1, l_prompt, N_HEADS, D_HEAD),
            lf[:l_prompt, :N_HEADS].reshape(1, 1, l_prompt, N_HEADS),
            state(hpre, hpim, 1),
            ks.reshape(1, n_sample, 1, N_HEADS, D_HEAD),
            vs.reshape(1, n_sample, 1, N_HEADS, D_HEAD),
            lf[smp, :N_HEADS].reshape(1, n_sample, 1, N_HEADS),
            state(hsre, hsim, n_sample))
```

```python
import functools
import math

import jax
import jax.numpy as jnp
from jax import lax
from jax.experimental import pallas as pl
from jax.experimental.pallas import tpu as pltpu

F32 = jnp.float32
BF16 = jnp.bfloat16

N_HEADS = 8
D_HEAD = 128
D_ATTN = N_HEADS * D_HEAD
SSM_GROUP = 16
P_STATE = 64
EPS = 1e-6
SAMPLE_OFF = 16
GROUPS_PER_BLOCK = 8
VMEM_LIMIT = 56 * 1024 * 1024
LOG2E = math.log2(math.e)


def _dot(a, b):
    return jnp.dot(a, b, preferred_element_type=F32)


def _split3(x):
    hi = x.astype(BF16)
    r1 = x - hi.astype(F32)
    mid = r1.astype(BF16)
    lo = (r1 - mid.astype(F32)).astype(BF16)
    return hi, mid, lo


def _tree(op, xs):
    xs = list(xs)
    while len(xs) > 1:
        xs = [op(xs[i], xs[i + 1]) for i in range(0, len(xs) - 1, 2)] + ([xs[-1]] if len(xs) % 2 else [])
    return xs[0]


def _log_sigmoid(f):
    return jnp.minimum(f, 0.0) - jnp.log1p(jnp.exp(-jnp.abs(f)))


def _head_rmsnorm(z, g):
    outs = []
    for h in range(N_HEADS):
        zh = z[:, h * D_HEAD:(h + 1) * D_HEAD]
        ms = jnp.mean(zh * zh, axis=-1, keepdims=True)
        outs.append(zh * lax.rsqrt(ms + EPS) * g)
    return jnp.concatenate(outs, axis=-1)


def _inproj_kernel(x_ref, g1_ref, w_ref, wf_ref, bf_ref, qg_ref, kg_ref,
                   q_ref, kf_ref, ks_ref, kb_ref, vf_ref, vs_ref, vb_ref, u_ref, sga_ref, sgs_ref, lf_ref,
                   h_scr, *, n_prompt_tiles, n_sample):
    i = pl.program_id(0)
    j = pl.program_id(1)
    smp = slice(SAMPLE_OFF, SAMPLE_OFF + n_sample)

    @pl.when(j == 0)
    def _():
        x = x_ref[...]
        ms = jnp.mean(x * x, axis=-1, keepdims=True)
        hb = ((x * lax.rsqrt(ms + EPS)) * g1_ref[...]).astype(BF16)
        h_scr[...] = hb
        lf_ref[...] = _log_sigmoid(_dot(hb, wf_ref[...]) + bf_ref[...])

    def proj():
        return _dot(h_scr[...], w_ref[...])

    def store_heads(val, full_ref, smp_ref):
        @pl.when(i < n_prompt_tiles)
        def _():
            for h in range(N_HEADS):
                full_ref[:, h, :] = val[:, h * D_HEAD:(h + 1) * D_HEAD]

        @pl.when(i == n_prompt_tiles)
        def _():
            for h in range(N_HEADS):
                smp_ref[:, h, :] = val[smp, h * D_HEAD:(h + 1) * D_HEAD]

    @pl.when(j == 0)
    def _():
        q_ref[...] = (_head_rmsnorm(proj(), qg_ref[...]) * (D_HEAD ** -0.5 * LOG2E)).astype(BF16)

    @pl.when(j == 1)
    def _():
        kn = _head_rmsnorm(proj(), kg_ref[...])
        kb_ref[...] = kn.astype(BF16)
        store_heads(kn, kf_ref, ks_ref)

    @pl.when(j == 2)
    def _():
        z = proj()
        vb_ref[...] = z.astype(BF16)
        store_heads(z, vf_ref, vs_ref)

    @pl.when(j == 3)
    def _():
        u_ref[...] = proj()

    for jj, ref, half in ((4, sga_ref, 0), (5, sga_ref, 1), (6, sgs_ref, 0), (7, sgs_ref, 1)):
        @pl.when(j == jj)
        def _(ref=ref, half=half):
            ref[:, half * 1024:(half + 1) * 1024] = jax.nn.sigmoid(proj()).astype(BF16)


def _inproj(x_all, g1, w1, wf, bf, qg, kg, *, tm, l_prompt, n_sample):
    rows, d = x_all.shape
    n_col = w1.shape[1] // 1024
    n_p = l_prompt // tm
    row_blk = lambda w: pl.BlockSpec((tm, w), lambda i, j: (i, 0))
    full = lambda a: pl.BlockSpec(a.shape, lambda i, j: (0,) * a.ndim)
    heads_p = pl.BlockSpec((tm, N_HEADS, D_HEAD), lambda i, j: (jnp.minimum(i, n_p - 1), 0, 0))
    heads_s = pl.BlockSpec((n_sample, N_HEADS, D_HEAD), lambda i, j: (0, 0, 0))
    f32_p = jax.ShapeDtypeStruct((l_prompt, N_HEADS, D_HEAD), F32)
    f32_s = jax.ShapeDtypeStruct((n_sample, N_HEADS, D_HEAD), F32)
    outs = [
        (jax.ShapeDtypeStruct((rows, D_ATTN), BF16), row_blk(D_ATTN)),
        (f32_p, heads_p), (f32_s, heads_s),
        (jax.ShapeDtypeStruct((rows, D_ATTN), BF16), row_blk(D_ATTN)),
        (f32_p, heads_p), (f32_s, heads_s),
        (jax.ShapeDtypeStruct((rows, D_ATTN), BF16), row_blk(D_ATTN)),
        (jax.ShapeDtypeStruct((rows, 1024), F32), row_blk(1024)),
        (jax.ShapeDtypeStruct((rows, d), BF16), row_blk(d)),
        (jax.ShapeDtypeStruct((rows, d), BF16), row_blk(d)),
        (jax.ShapeDtypeStruct((rows, 128), F32), row_blk(128)),
    ]
    kernel = functools.partial(_inproj_kernel, n_prompt_tiles=n_p, n_sample=n_sample)
    return pl.pallas_call(
        kernel,
        grid=(rows // tm, n_col),
        in_specs=[row_blk(d), full(g1), pl.BlockSpec((d, 1024), lambda i, j: (0, j)),
                  full(wf), full(bf), full(qg), full(kg)],
        out_specs=[o[1] for o in outs],
        out_shape=[o[0] for o in outs],
        scratch_shapes=[pltpu.VMEM((tm, d), BF16)],
        compiler_params=pltpu.CompilerParams(
            dimension_semantics=("arbitrary", "arbitrary"), vmem_limit_bytes=VMEM_LIMIT),
        name="inproj",
    )(x_all, g1, w1, wf, bf, qg, kg)


def _cumsum_kernel(lf_ref, crow_ref, carry_scr):
    @pl.when(pl.program_id(0) == 0)
    def _():
        carry_scr[...] = jnp.zeros_like(carry_scr)

    lf = lf_ref[...]
    t = lf.shape[0]
    r = lax.broadcasted_iota(jnp.int32, (t, t), 0)
    c = lax.broadcasted_iota(jnp.int32, (t, t), 1)
    tri = (c <= r).astype(BF16)
    hi, mid, lo = _split3(lf)
    cs = _dot(tri, hi) + _dot(tri, mid) + _dot(tri, lo) + carry_scr[...]
    carry_scr[...] = cs[t - 1:t, :]
    crow_ref[...] = (cs * LOG2E).T[:N_HEADS, :]


def _cumsum_rows(lf, *, n_rows, t):
    return pl.pallas_call(
        _cumsum_kernel,
        grid=(n_rows // t,),
        in_specs=[pl.BlockSpec((t, 128), lambda i: (i, 0))],
        out_specs=pl.BlockSpec((N_HEADS, t), lambda i: (0, i)),
        out_shape=jax.ShapeDtypeStruct((N_HEADS, n_rows), F32),
        scratch_shapes=[pltpu.VMEM((1, 128), F32)],
        compiler_params=pltpu.CompilerParams(dimension_semantics=("arbitrary",)),
        name="logf_cumsum",
    )(lf)


def _flash_kernel(q_ref, k_ref, v_ref, c_ref, zero_ref, o_ref,
                  s_scr, p_scr, m_scr, l_scr, a_scr, acc_scr, *, t, rc):
    del zero_ref
    i = pl.program_id(1)
    q = q_ref[...]
    m_scr[...] = jnp.full_like(m_scr, -jnp.inf)
    l_scr[...] = jnp.zeros_like(l_scr)
    acc_scr[...] = jnp.zeros_like(acc_scr)

    def scores(j, slot):
        k0 = pl.multiple_of(j * t, t)
        s_scr[slot] = lax.dot_general(q, k_ref[pl.ds(k0, t), :], (((1,), (1,)), ((), ())),
                                      preferred_element_type=F32)

    def weighted_values(j, slot):
        k0 = pl.multiple_of(j * t, t)
        acc_scr[...] = a_scr[slot] * acc_scr[...] + _dot(p_scr[slot], v_ref[pl.ds(k0, t), :])

    def softmax(j, slot, masked):
        cj = c_ref[j]

        def chunk(r, carry):
            rows = pl.ds(pl.multiple_of(r * rc, rc), rc)
            s = s_scr[slot, rows, :] - cj
            if masked:
                row = r * rc + lax.broadcasted_iota(jnp.int32, (rc, t), 0)
                col = lax.broadcasted_iota(jnp.int32, (rc, t), 1)
                s = jnp.where(col <= row, s, -jnp.inf)
            m_prev = m_scr[rows, :]
            m_new = jnp.maximum(m_prev, jnp.broadcast_to(jnp.max(s, axis=-1, keepdims=True), (rc, D_HEAD)))
            alpha = jnp.exp2(m_prev - m_new)
            p = jnp.exp2(s - jnp.tile(m_new, (1, t // D_HEAD)))
            l_scr[rows, :] = alpha * l_scr[rows, :] + jnp.broadcast_to(
                jnp.sum(p, axis=-1, keepdims=True), (rc, D_HEAD))
            m_scr[rows, :] = m_new
            a_scr[slot, rows, :] = alpha
            p_scr[slot, rows, :] = p.astype(BF16)
            return carry

        lax.fori_loop(0, t // rc, chunk, 0, unroll=True)

    p_scr[1] = jnp.zeros((t, t), BF16)
    a_scr[1] = jnp.ones((t, D_HEAD), F32)
    scores(0, 0)

    def pair(jj, carry):
        j = 2 * jj

        @pl.when(jj >= 0)
        def _():
            scores(j + 1, 1)
            softmax(j, 0, False)
            weighted_values(jnp.maximum(j - 1, 0), 1)

        @pl.when(jj >= 0)
        def _():
            scores(j + 2, 0)
            softmax(j + 1, 1, False)
            weighted_values(j, 0)

        return carry

    n_pairs = i // 2
    lax.fori_loop(0, n_pairs, pair, 0)
    prev = jnp.maximum(2 * n_pairs - 1, 0)

    @pl.when(i % 2 == 0)
    def _():
        softmax(i, 0, True)
        weighted_values(prev, 1)
        weighted_values(i, 0)

    @pl.when(i % 2 == 1)
    def _():
        scores(i, 1)
        softmax(i - 1, 0, False)
        weighted_values(prev, 1)

    @pl.when(i % 2 == 1)
    def _():
        softmax(i, 1, True)
        weighted_values(i - 1, 0)
        weighted_values(i, 1)

    o_ref[...] = (acc_scr[...] * (1.0 / l_scr[...])).astype(o_ref.dtype)


def _flash(q, kb, vb, c_tiles, *, t, rc):
    rows = q.shape[0]
    nk = c_tiles.shape[1]
    kernel = functools.partial(_flash_kernel, t=t, rc=rc)
    return pl.pallas_call(
        kernel,
        grid=(N_HEADS, nk),
        in_specs=[pl.BlockSpec((t, D_HEAD), lambda h, i: (i, h)),
                  pl.BlockSpec((nk * t, D_HEAD), lambda h, i: (0, h)),
                  pl.BlockSpec((nk * t, D_HEAD), lambda h, i: (0, h)),
                  pl.BlockSpec((None, nk, 1, t), lambda h, i: (h, 0, 0, 0)),
                  pl.BlockSpec(memory_space=pl.ANY)],
        out_specs=pl.BlockSpec((t, D_HEAD), lambda h, i: (i, h)),
        out_shape=jax.ShapeDtypeStruct((rows, D_ATTN), BF16),
        input_output_aliases={4: 0},
        scratch_shapes=[pltpu.VMEM((2, t, t), F32), pltpu.VMEM((2, t, t), BF16),
                        pltpu.VMEM((t, D_HEAD), F32), pltpu.VMEM((t, D_HEAD), F32),
                        pltpu.VMEM((2, t, D_HEAD), F32), pltpu.VMEM((t, D_HEAD), F32)],
        compiler_params=pltpu.CompilerParams(
            dimension_semantics=("arbitrary", "arbitrary"), vmem_limit_bytes=VMEM_LIMIT),
        name="fox_prompt",
    )(q, kb, vb, c_tiles, jnp.zeros((rows, D_ATTN), BF16))


def _decode_kernel(pt_ref, q_ref, kc_ref, vc_ref, lfc_ref, *refs, pp, group):
    del pt_ref
    k_refs = refs[:pp]
    v_refs = refs[pp:2 * pp]
    lf_refs = refs[2 * pp:3 * pp]
    o_ref = refs[3 * pp]
    m_scr, l_scr, acc_scr, carry_scr, wide_scr = refs[3 * pp + 1:]
    g = pl.program_id(1)
    n_g = pl.num_programs(1)
    page = lf_refs[0].shape[-1]

    q = q_ref[0]
    lane = lax.broadcasted_iota(jnp.int32, (N_HEADS, 128), 1)
    sub = lax.broadcasted_iota(jnp.int32, (N_HEADS, 128), 0)
    diag = lane == sub
    later = (lax.broadcasted_iota(jnp.int32, (page, page), 0)
             > lax.broadcasted_iota(jnp.int32, (page, page), 1)).astype(F32)

    @pl.when(g == 0)
    def _():
        m_scr[...] = jnp.full_like(m_scr, -jnp.inf)
        l_scr[...] = jnp.zeros_like(l_scr)
        acc_scr[...] = jnp.zeros_like(acc_scr)
        cur = jnp.where(diag, jnp.broadcast_to(lfc_ref[0], (N_HEADS, 128)), 0.0)
        carry_scr[...] = jnp.broadcast_to(jnp.sum(cur, axis=-1, keepdims=True) * LOG2E, (N_HEADS, 128))

    m_run = m_scr[...]
    l_run = l_scr[...]
    acc_run = acc_scr[...]
    for pi in range(pp):
        x = lf_refs[pi][...] * LOG2E
        hi, mid, lo = _split3(x)
        carry = carry_scr[...]
        bias = (_dot(hi.astype(F32), later) + _dot(mid.astype(F32), later)
                + _dot(lo.astype(F32), later) + carry)
        carry_scr[...] = carry + jnp.sum(x, axis=-1, keepdims=True)
        wide_scr[pi] = jnp.concatenate([bias, jnp.zeros((page - N_HEADS, page), F32)], axis=0).T
        k_ref, v_ref = k_refs[pi], v_refs[pi]
        for t0 in range(0, page, group):
            ss = []
            for t in range(t0, t0 + group):
                b_t = jnp.where(diag, jnp.broadcast_to(wide_scr[pi, t:t + 1, :], (N_HEADS, 128)), 0.0)
                s_t = jnp.sum(k_ref[t] * q + b_t, axis=-1, keepdims=True)
                ss.append(jnp.broadcast_to(s_t, (N_HEADS, 128)))
            m_new = jnp.maximum(m_run, _tree(jnp.maximum, ss))
            alpha = jnp.exp2(m_run - m_new)
            ps = [jnp.exp2(s_t - m_new) for s_t in ss]
            l_run = alpha * l_run + _tree(jnp.add, ps)
            acc_run = alpha * acc_run + _tree(jnp.add, [p_t * v_ref[t0 + n] for n, p_t in enumerate(ps)])
            m_run = m_new

    m_scr[...] = m_run
    l_scr[...] = l_run
    acc_scr[...] = acc_run

    @pl.when(g == n_g - 1)
    def _():
        s_cur = jnp.sum(kc_ref[0] * q, axis=-1, keepdims=True)
        m_new = jnp.maximum(m_run, s_cur)
        alpha = jnp.exp2(m_run - m_new)
        p = jnp.exp2(s_cur - m_new)
        l = alpha * l_run + p
        acc = alpha * acc_run + p * vc_ref[0]
        o_ref[0] = acc * (1.0 / l)


def _decode(page_table, q_s, k_cur, v_cur, lf_cur, cache_k, cache_v, cache_lft, *, pp, group):
    b, n_pages = page_table.shape
    page = cache_lft.shape[-1]
    n_g = n_pages // pp
    kernel = functools.partial(_decode_kernel, pp=pp, group=group)
    seq_blk = lambda w: pl.BlockSpec((1, w, 128), lambda s, g, pt: (s, 0, 0))

    def page_map(pi, nd):
        def index_map(s, g, pt):
            return (pt[s * n_pages + (n_pages - 1 - (g * pp + pi))],) + (0,) * nd
        return index_map

    k_specs = [pl.BlockSpec((None, page, N_HEADS, D_HEAD), page_map(pi, 3)) for pi in range(pp)]
    v_specs = [pl.BlockSpec((None, page, N_HEADS, D_HEAD), page_map(pi, 3)) for pi in range(pp)]
    lf_specs = [pl.BlockSpec((None, N_HEADS, page), page_map(pi, 2)) for pi in range(pp)]
    grid_spec = pltpu.PrefetchScalarGridSpec(
        num_scalar_prefetch=1,
        grid=(b, n_g),
        in_specs=[seq_blk(N_HEADS), seq_blk(N_HEADS), seq_blk(N_HEADS), seq_blk(1)]
                 + k_specs + v_specs + lf_specs,
        out_specs=pl.BlockSpec((1, N_HEADS, D_HEAD), lambda s, g, pt: (s, 0, 0)),
        scratch_shapes=[pltpu.VMEM((N_HEADS, 128), F32), pltpu.VMEM((N_HEADS, 128), F32),
                        pltpu.VMEM((N_HEADS, D_HEAD), F32), pltpu.VMEM((N_HEADS, 128), F32),
                        pltpu.VMEM((pp, page, page), F32)],
    )
    return pl.pallas_call(
        kernel,
        grid_spec=grid_spec,
        out_shape=jax.ShapeDtypeStruct((b, N_HEADS, D_HEAD), F32),
        compiler_params=pltpu.CompilerParams(
            dimension_semantics=("arbitrary", "arbitrary"), vmem_limit_bytes=VMEM_LIMIT),
        name="fox_decode",
    )(page_table.reshape(-1), q_s, k_cur, v_cur, lf_cur,
      *([cache_k] * pp), *([cache_v] * pp), *([cache_lft] * pp))


def _gelu_tanh(y):
    return 0.5 * y * (1.0 + jnp.tanh(math.sqrt(2.0 / math.pi) * (y + 0.044715 * (y * y * y))))


def _ssm_kernel(u_ref, bre_ref, bim_ref, cre_ref, cim_ref,
                are_ref, aim_ref, akre_ref, akim_ref, d_ref, wg_ref, bg_ref, h0re_ref, h0im_ref,
                o_ref, hpre_ref, hpim_ref, hsre_ref, hsim_ref,
                sre_scr, sim_scr, endre_scr, endim_scr, hinre_scr, hinim_scr, cre_scr, cim_scr,
                *, n_prompt_chunks, steps, slabs_per_pass, n_sample):
    c = pl.program_id(0)
    n_slab = sre_scr.shape[0]
    n_blk = bre_ref.shape[0]
    blk_slabs = n_slab // n_blk
    lanes = lambda sb: slice(sb * 128, (sb + 1) * 128)

    def input_proj(ub):
        for j in range(n_blk):
            uj = ub[:, j * 128:(j + 1) * 128]
            zr = _dot(uj, bre_ref[j])
            zi = _dot(uj, bim_ref[j])
            for i in range(blk_slabs):
                sre_scr[j * blk_slabs + i] = zr[:, lanes(i)]
                sim_scr[j * blk_slabs + i] = zi[:, lanes(i)]

    def tail(uf):
        ys = []
        for j in range(n_blk):
            sbs = range(j * blk_slabs, (j + 1) * blk_slabs)
            hr = jnp.concatenate([sre_scr[sb] for sb in sbs], axis=-1).astype(BF16)
            hi = jnp.concatenate([sim_scr[sb] for sb in sbs], axis=-1).astype(BF16)
            ys.append(_dot(hr, cre_ref[j]) + _dot(hi, cim_ref[j]))
        y = jnp.concatenate(ys, axis=-1) + d_ref[...] * uf
        gl = _gelu_tanh(y)
        return gl * jax.nn.sigmoid(_dot(gl.astype(BF16), wg_ref[...]) + bg_ref[...])

    def scan_pass(store):
        for g0 in range(0, n_slab, slabs_per_pass):
            sbs = list(range(g0, g0 + slabs_per_pass))
            ar = [jnp.broadcast_to(are_ref[:, lanes(sb)], (8, 128)) for sb in sbs]
            ai = [jnp.broadcast_to(aim_ref[:, lanes(sb)], (8, 128)) for sb in sbs]
            if store:
                init = ([hinre_scr[:, lanes(sb)] for sb in sbs], [hinim_scr[:, lanes(sb)] for sb in sbs])
            else:
                zero = jnp.zeros((8, 128), F32)
                init = ([zero] * len(sbs), [zero] * len(sbs))

            def body(k, carry, sbs=sbs, ar=ar, ai=ai):
                re, im = carry
                rows = pl.ds(k, 8, stride=steps)
                nre, nim = [], []
                for n, sb in enumerate(sbs):
                    r = ar[n] * re[n] - ai[n] * im[n] + sre_scr[sb, rows, :]
                    i = ar[n] * im[n] + ai[n] * re[n] + sim_scr[sb, rows, :]
                    if store:
                        sre_scr[sb, rows, :] = r
                        sim_scr[sb, rows, :] = i
                    nre.append(r)
                    nim.append(i)
                return nre, nim

            re, im = lax.fori_loop(0, steps, body, init)
            if not store:
                for n, sb in enumerate(sbs):
                    endre_scr[:, lanes(sb)] = re[n]
                    endim_scr[:, lanes(sb)] = im[n]

    @pl.when(c == 0)
    def _():
        cre_scr[...] = jnp.zeros_like(cre_scr)
        cim_scr[...] = jnp.zeros_like(cim_scr)

    @pl.when(c < n_prompt_chunks)
    def _():
        uf = u_ref[...]
        input_proj(uf.astype(BF16))
        scan_pass(False)
        cr = cre_scr[...]
        ci = cim_scr[...]
        akr = akre_ref[...]
        aki = akim_ref[...]
        for s in range(8):
            hinre_scr[s:s + 1, :] = cr
            hinim_scr[s:s + 1, :] = ci
            er = endre_scr[s:s + 1, :]
            ei = endim_scr[s:s + 1, :]
            cr, ci = er + (akr * cr - aki * ci), ei + (akr * ci + aki * cr)
        cre_scr[...] = cr
        cim_scr[...] = ci
        hpre_ref[...] = cr
        hpim_ref[...] = ci
        scan_pass(True)
        o_ref[...] = tail(uf).astype(o_ref.dtype)

    @pl.when(c == n_prompt_chunks)
    def _():
        uf = u_ref[...]
        input_proj(uf.astype(BF16))
        rs = slice(SAMPLE_OFF, SAMPLE_OFF + n_sample)
        for sb in range(n_slab):
            ar = are_ref[:, lanes(sb)]
            ai = aim_ref[:, lanes(sb)]
            h0r = h0re_ref[:, lanes(sb)]
            h0i = h0im_ref[:, lanes(sb)]
            nre = ar * h0r - ai * h0i + sre_scr[sb, rs, :]
            nim = ar * h0i + ai * h0r + sim_scr[sb, rs, :]
            sre_scr[sb, rs, :] = nre
            sim_scr[sb, rs, :] = nim
            hsre_ref[:, lanes(sb)] = nre
            hsim_ref[:, lanes(sb)] = nim
        o_ref[...] = tail(uf).astype(o_ref.dtype)


def _ssm(u, bre, bim, cre, cim, are, aim, akre, akim, d_skip, w_glu, b_glu, h0re, h0im,
         *, t, n_prompt_chunks, slabs_per_pass):
    rows = u.shape[0]
    n_state = are.shape[1]
    n_sample = h0re.shape[0]
    full = lambda a: pl.BlockSpec(a.shape, lambda c: (0,) * a.ndim)
    kernel = functools.partial(_ssm_kernel, n_prompt_chunks=n_prompt_chunks, steps=t // 8,
                               slabs_per_pass=slabs_per_pass, n_sample=n_sample)
    state_row = jax.ShapeDtypeStruct((1, n_state), F32)
    state_smp = jax.ShapeDtypeStruct((n_sample, n_state), F32)
    consts = (bre, bim, cre, cim, are, aim, akre, akim, d_skip, w_glu, b_glu, h0re, h0im)
    return pl.pallas_call(
        kernel,
        grid=(n_prompt_chunks + 1,),
        in_specs=[pl.BlockSpec((t, u.shape[1]), lambda c: (c, 0))] + [full(a) for a in consts],
        out_specs=[pl.BlockSpec((t, u.shape[1]), lambda c: (c, 0)),
                   full(state_row), full(state_row), full(state_smp), full(state_smp)],
        out_shape=[jax.ShapeDtypeStruct((rows, u.shape[1]), BF16),
                   state_row, state_row, state_smp, state_smp],
        scratch_shapes=[pltpu.VMEM((n_state // 128, t, 128), F32), pltpu.VMEM((n_state // 128, t, 128), F32),
                        pltpu.VMEM((8, n_state), F32), pltpu.VMEM((8, n_state), F32),
                        pltpu.VMEM((8, n_state), F32), pltpu.VMEM((8, n_state), F32),
                        pltpu.VMEM((1, n_state), F32), pltpu.VMEM((1, n_state), F32)],
        compiler_params=pltpu.CompilerParams(
            dimension_semantics=("arbitrary",), vmem_limit_bytes=VMEM_LIMIT),
        name="s5_branch",
    )(u, *consts)


def _merge_kernel(attn_ref, as_ref, ssm_ref, sga_ref, sgs_ref, x_ref, wba_ref, wbs_ref, wo_ref, g2_ref,
                  x1_ref, h2_ref, ma_scr, *, sample_tile):
    n_sample = as_ref.shape[0]
    ma_scr[...] = _dot(attn_ref[...], wba_ref[...])

    @pl.when(pl.program_id(0) == sample_tile)
    def _():
        acc = jnp.zeros((n_sample, wba_ref.shape[1]), F32)
        for h in range(N_HEADS):
            acc = acc + _dot(as_ref[:, h, :].astype(BF16), wba_ref[h * D_HEAD:(h + 1) * D_HEAD, :])
        ma_scr[SAMPLE_OFF:SAMPLE_OFF + n_sample, :] = acc

    m = (sga_ref[...].astype(F32) * ma_scr[...]
         + sgs_ref[...].astype(F32) * _dot(ssm_ref[...], wbs_ref[...]))
    x1 = x_ref[...] + _dot(m.astype(BF16), wo_ref[...])
    x1_ref[...] = x1
    ms = jnp.mean(x1 * x1, axis=-1, keepdims=True)
    h2_ref[...] = ((x1 * lax.rsqrt(ms + EPS)) * g2_ref[...]).astype(BF16)


def _merge(attn, attn_s, ssm, sga, sgs, x_all, wba, wbs, wo, g2, *, tm, sample_tile):
    rows, d = x_all.shape
    row_blk = lambda w: pl.BlockSpec((tm, w), lambda i: (i, 0))
    const = lambda a: pl.BlockSpec(a.shape, lambda i: (0,) * a.ndim, pipeline_mode=pl.Buffered(1))
    kernel = functools.partial(_merge_kernel, sample_tile=sample_tile)
    return pl.pallas_call(
        kernel,
        grid=(rows // tm,),
        in_specs=[row_blk(D_ATTN), const(attn_s), row_blk(ssm.shape[1]), row_blk(d), row_blk(d), row_blk(d),
                  const(wba), const(wbs), const(wo), const(g2)],
        out_specs=[row_blk(d), row_blk(d)],
        out_shape=[jax.ShapeDtypeStruct((rows, d), F32), jax.ShapeDtypeStruct((rows, d), BF16)],
        scratch_shapes=[pltpu.VMEM((tm, d), F32)],
        compiler_params=pltpu.CompilerParams(
            dimension_semantics=("arbitrary",), vmem_limit_bytes=VMEM_LIMIT),
        name="merge",
    )(attn, attn_s, ssm, sga, sgs, x_all, wba, wbs, wo, g2)


def _ffn_kernel(h2_ref, x1_ref, w1_ref, w3_ref, w2_ref, o_ref):
    @pl.when(pl.program_id(1) == 0)
    def _():
        o_ref[...] = x1_ref[...]

    h2 = h2_ref[...]
    a = _dot(h2, w1_ref[...])
    b = _dot(h2, w3_ref[...])
    o_ref[...] += _dot(((a * jax.nn.sigmoid(a)) * b).astype(BF16), w2_ref[...])


def _ffn(h2, x1, w1, w3, w2, *, tm, tf):
    rows, d = x1.shape
    d_ff = w1.shape[1]
    return pl.pallas_call(
        _ffn_kernel,
        grid=(rows // tm, d_ff // tf),
        in_specs=[pl.BlockSpec((tm, d), lambda i, f: (i, 0)),
                  pl.BlockSpec((tm, d), lambda i, f: (i, 0)),
                  pl.BlockSpec((d, tf), lambda i, f: (0, f)),
                  pl.BlockSpec((d, tf), lambda i, f: (0, f)),
                  pl.BlockSpec((tf, d), lambda i, f: (f, 0))],
        out_specs=pl.BlockSpec((tm, d), lambda i, f: (i, 0)),
        out_shape=jax.ShapeDtypeStruct((rows, d), F32),
        compiler_params=pltpu.CompilerParams(
            dimension_semantics=("arbitrary", "arbitrary"), vmem_limit_bytes=VMEM_LIMIT),
        name="ffn",
    )(h2, x1, w1, w3, w2)


def _ssm_tables(lam_re, lam_im, log_dt, b_re, b_im, c_re, c_im, steps):
    n_groups = lam_re.shape[0]
    lr, li = lam_re.astype(F32), lam_im.astype(F32)
    dt = jnp.exp(log_dt.astype(F32))[:, None]
    mag = jnp.exp(lr * dt)
    ar, ai = mag * jnp.cos(li * dt), mag * jnp.sin(li * dt)
    den = lr * lr + li * li
    cr = ((ar - 1.0) * lr + ai * li) / den
    ci = (ai * lr - (ar - 1.0) * li) / den
    br, bi = b_re.astype(F32), b_im.astype(F32)
    bbr = cr[..., None] * br - ci[..., None] * bi
    bbi = cr[..., None] * bi + ci[..., None] * br
    akr, aki = ar, ai
    for _ in range(steps - 1):
        akr, aki = akr * ar - aki * ai, akr * ai + aki * ar
    n_blk = n_groups // GROUPS_PER_BLOCK
    eye = jnp.eye(GROUPS_PER_BLOCK, dtype=F32)

    def in_blocks(m):
        m = m.reshape(n_blk, GROUPS_PER_BLOCK, P_STATE, SSM_GROUP)
        return jnp.einsum('jgpc,gh->jgchp', m, eye).reshape(
            n_blk, GROUPS_PER_BLOCK * SSM_GROUP, GROUPS_PER_BLOCK * P_STATE).astype(BF16)

    def out_blocks(m):
        m = m.reshape(n_blk, GROUPS_PER_BLOCK, SSM_GROUP, P_STATE)
        return jnp.einsum('jgcp,gh->jgphc', m, eye).reshape(
            n_blk, GROUPS_PER_BLOCK * P_STATE, GROUPS_PER_BLOCK * SSM_GROUP).astype(BF16)

    flat = lambda a: a.reshape(1, -1)
    return (in_blocks(bbr), in_blocks(bbi),
            out_blocks(c_re.astype(F32)), out_blocks(-c_im.astype(F32)),
            flat(ar), flat(ai), flat(akr), flat(aki))


def _tiles(l_prompt):
    if l_prompt == 8208:
        return dict(t_ssm=432, t_attn=640, rc=32, tm_ffn=864, tf=512, slabs_per_pass=4, pp=8, group=32)
    return dict(t_ssm=200, t_attn=128, rc=32, tm_ffn=300, tf=512, slabs_per_pass=4, pp=2, group=32)


def kernel(x_prompt, x_sample, cache_k, cache_v, cache_logf, state_ssm, page_table, meta, norm1_g, w_in, b_f, q_norm_g, k_norm_g, lam_re, lam_im, log_dt, b_re, b_im, c_re, c_im, d_skip, w_glu, b_glu, w_br_attn, w_br_ssm, w_out, norm2_g, w_ff1, w_ff3, w_ff2):
    assert x_prompt.shape[0] == 1 and x_sample.shape[1] == 1 and w_in.shape[0] == 1
    n_meta, d = meta.shape
    seq = x_prompt.shape[1]
    n_sample = x_sample.shape[0]
    l_prompt = n_meta + seq
    cfg = _tiles(l_prompt)
    t_ssm = cfg["t_ssm"]
    assert l_prompt % t_ssm == 0 and SAMPLE_OFF + n_sample <= t_ssm
    n_p = l_prompt // t_ssm
    rows = l_prompt + t_ssm
    s0 = l_prompt + SAMPLE_OFF
    d_ssm = w_glu.shape[1]
    n_groups = d_ssm // SSM_GROUP

    x_all = jnp.concatenate([
        meta.astype(F32), x_prompt[0], jnp.zeros((SAMPLE_OFF, d), F32), x_sample[:, 0],
        jnp.zeros((rows - s0 - n_sample, d), F32)], axis=0)
    w = w_in[0]
    c0, c1 = 3 * D_ATTN, 3 * D_ATTN + N_HEADS
    n_w1 = w.shape[1] - N_HEADS
    col = lax.broadcasted_iota(jnp.int32, (1, n_w1), 1)
    w1 = jnp.where(col < c0, w[:, :n_w1], w[:, N_HEADS:]).astype(BF16)
    wf = jnp.pad(w[:, c0:c1], ((0, 0), (0, 128 - N_HEADS))).astype(BF16)
    bf = jnp.pad(b_f[0].astype(F32), (0, 128 - N_HEADS)).reshape(1, 128)
    row = lambda a: a.astype(F32).reshape(1, -1)

    q, kf, ks, kb, vf, vs, vb, u, sga, sgs, lf = _inproj(
        x_all, row(norm1_g[0]), w1, wf, bf, row(q_norm_g[0]), row(k_norm_g[0]),
        tm=t_ssm, l_prompt=l_prompt, n_sample=n_sample)

    ta = cfg["t_attn"]
    nk = -(-l_prompt // ta)
    assert nk * ta <= rows
    c_rows = _cumsum_rows(lf, n_rows=nk * ta, t=ta)
    attn = _flash(q, kb, vb, c_rows.reshape(N_HEADS, nk, 1, ta), t=ta, rc=cfg["rc"])

    smp = slice(s0, s0 + n_sample)
    q_s = q[smp].astype(F32).reshape(n_sample, N_HEADS, D_HEAD)
    lf_cur = lf[smp].reshape(n_sample, 1, 128)
    attn_s = _decode(page_table, q_s, ks, vs, lf_cur, cache_k[0], cache_v[0],
                     jnp.swapaxes(cache_logf[0], 1, 2), pp=cfg["pp"], group=cfg["group"])

    steps = t_ssm // 8
    bre, bim, cre, cim, are, aim, akre, akim = _ssm_tables(
        lam_re[0], lam_im[0], log_dt[0], b_re[0], b_im[0], c_re[0], c_im[0], steps)
    h0 = state_ssm[0].astype(F32).reshape(n_sample, n_groups * P_STATE, 2)
    ssm, hpre, hpim, hsre, hsim = _ssm(
        u, bre, bim, cre, cim, are, aim, akre, akim,
        row(d_skip[0]), w_glu[0].astype(BF16), row(b_glu[0]), h0[..., 0], h0[..., 1],
        t=t_ssm, n_prompt_chunks=n_p, slabs_per_pass=cfg["slabs_per_pass"])

    x1, h2 = _merge(attn, attn_s, ssm, sga, sgs, x_all, w_br_attn[0].astype(BF16),
                    w_br_ssm[0].astype(BF16), w_out[0].astype(BF16), row(norm2_g[0]),
                    tm=t_ssm, sample_tile=n_p)
    y = _ffn(h2, x1, w_ff1[0].astype(BF16), w_ff3[0].astype(BF16), w_ff2[0].astype(BF16),
             tm=cfg["tm_ffn"], tf=cfg["tf"])

    state = lambda re, im, n: jnp.stack([re, im], axis=-1).reshape(1, n, n_groups, P_STATE, 2)
    return (y[n_meta:l_prompt][None],
            y[smp][:, None],
            kf.reshape(1, 1, l_prompt, N_HEADS, D_HEAD),
            vf.reshape(1, 1, l_prompt, N_HEADS, D_HEAD),
            lf[:l_prompt, :N_HEADS].reshape(1, 1, l_prompt, N_HEADS),
            state(hpre, hpim, 1),
            ks.reshape(1, n_sample, 1, N_HEADS, D_HEAD),
            vs.reshape(1, n_sample, 1, N_HEADS, D_HEAD),
            lf[smp, :N_HEADS].reshape(1, n_sample, 1, N_HEADS),
            state(hsre, hsim, n_sample))
```

```python
import functools
import math

import jax
import jax.numpy as jnp
from jax import lax
from jax.experimental import pallas as pl
from jax.experimental.pallas import tpu as pltpu

F32 = jnp.float32
BF16 = jnp.bfloat16

N_HEADS = 8
D_HEAD = 128
D_ATTN = N_HEADS * D_HEAD
SSM_GROUP = 16
P_STATE = 64
EPS = 1e-6
SAMPLE_OFF = 16
GROUPS_PER_BLOCK = 8
VMEM_LIMIT = 56 * 1024 * 1024
LOG2E = math.log2(math.e)


def _dot(a, b):
    return jnp.dot(a, b, preferred_element_type=F32)


def _split3(x):
    hi = x.astype(BF16)
    r1 = x - hi.astype(F32)
    mid = r1.astype(BF16)
    lo = (r1 - mid.astype(F32)).astype(BF16)
    return hi, mid, lo


def _tree(op, xs):
    xs = list(xs)
    while len(xs) > 1:
        xs = [op(xs[i], xs[i + 1]) for i in range(0, len(xs) - 1, 2)] + ([xs[-1]] if len(xs) % 2 else [])
    return xs[0]


def _log_sigmoid(f):
    return jnp.minimum(f, 0.0) - jnp.log1p(jnp.exp(-jnp.abs(f)))


def _head_rmsnorm(z, g):
    outs = []
    for h in range(N_HEADS):
        zh = z[:, h * D_HEAD:(h + 1) * D_HEAD]
        ms = jnp.mean(zh * zh, axis=-1, keepdims=True)
        outs.append(zh * lax.rsqrt(ms + EPS) * g)
    return jnp.concatenate(outs, axis=-1)


def _inproj_kernel(x_ref, g1_ref, w_ref, wf_ref, bf_ref, qg_ref, kg_ref,
                   q_ref, kf_ref, ks_ref, kb_ref, vf_ref, vs_ref, vb_ref, u_ref, sga_ref, sgs_ref, lf_ref,
                   h_scr, *, n_prompt_tiles, n_sample):
    i = pl.program_id(0)
    j = pl.program_id(1)
    smp = slice(SAMPLE_OFF, SAMPLE_OFF + n_sample)

    @pl.when(j == 0)
    def _():
        x = x_ref[...]
        ms = jnp.mean(x * x, axis=-1, keepdims=True)
        hb = ((x * lax.rsqrt(ms + EPS)) * g1_ref[...]).astype(BF16)
        h_scr[...] = hb
        lf_ref[...] = _log_sigmoid(_dot(hb, wf_ref[...]) + bf_ref[...])

    def proj():
        return _dot(h_scr[...], w_ref[...])

    def store_heads(val, full_ref, smp_ref):
        @pl.when(i < n_prompt_tiles)
        def _():
            for h in range(N_HEADS):
                full_ref[:, h, :] = val[:, h * D_HEAD:(h + 1) * D_HEAD]

        @pl.when(i == n_prompt_tiles)
        def _():
            for h in range(N_HEADS):
                smp_ref[:, h, :] = val[smp, h * D_HEAD:(h + 1) * D_HEAD]

    @pl.when(j == 0)
    def _():
        q_ref[...] = (_head_rmsnorm(proj(), qg_ref[...]) * (D_HEAD ** -0.5 * LOG2E)).astype(BF16)

    @pl.when(j == 1)
    def _():
        kn = _head_rmsnorm(proj(), kg_ref[...])
        kb_ref[...] = kn.astype(BF16)
        store_heads(kn, kf_ref, ks_ref)

    @pl.when(j == 2)
    def _():
        z = proj()
        vb_ref[...] = z.astype(BF16)
        store_heads(z, vf_ref, vs_ref)

    @pl.when(j == 3)
    def _():
        u_ref[...] = proj()

    for jj, ref, half in ((4, sga_ref, 0), (5, sga_ref, 1), (6, sgs_ref, 0), (7, sgs_ref, 1)):
        @pl.when(j == jj)
        def _(ref=ref, half=half):
            ref[:, half * 1024:(half + 1) * 1024] = jax.nn.sigmoid(proj()).astype(BF16)


def _inproj(x_all, g1, w1, wf, bf, qg, kg, *, tm, l_prompt, n_sample):
    rows, d = x_all.shape
    n_col = w1.shape[1] // 1024
    n_p = l_prompt // tm
    row_blk = lambda w: pl.BlockSpec((tm, w), lambda i, j: (i, 0))
    full = lambda a: pl.BlockSpec(a.shape, lambda i, j: (0,) * a.ndim)
    heads_p = pl.BlockSpec((tm, N_HEADS, D_HEAD), lambda i, j: (jnp.minimum(i, n_p - 1), 0, 0))
    heads_s = pl.BlockSpec((n_sample, N_HEADS, D_HEAD), lambda i, j: (0, 0, 0))
    f32_p = jax.ShapeDtypeStruct((l_prompt, N_HEADS, D_HEAD), F32)
    f32_s = jax.ShapeDtypeStruct((n_sample, N_HEADS, D_HEAD), F32)
    outs = [
        (jax.ShapeDtypeStruct((rows, D_ATTN), BF16), row_blk(D_ATTN)),
        (f32_p, heads_p), (f32_s, heads_s),
        (jax.ShapeDtypeStruct((rows, D_ATTN), BF16), row_blk(D_ATTN)),
        (f32_p, heads_p), (f32_s, heads_s),
        (jax.ShapeDtypeStruct((rows, D_ATTN), BF16), row_blk(D_ATTN)),
        (jax.ShapeDtypeStruct((rows, 1024), F32), row_blk(1024)),
        (jax.ShapeDtypeStruct((rows, d), BF16), row_blk(d)),
        (jax.ShapeDtypeStruct((rows, d), BF16), row_blk(d)),
        (jax.ShapeDtypeStruct((rows, 128), F32), row_blk(128)),
    ]
    kernel = functools.partial(_inproj_kernel, n_prompt_tiles=n_p, n_sample=n_sample)
    return pl.pallas_call(
        kernel,
        grid=(rows // tm, n_col),
        in_specs=[row_blk(d), full(g1), pl.BlockSpec((d, 1024), lambda i, j: (0, j)),
                  full(wf), full(bf), full(qg), full(kg)],
        out_specs=[o[1] for o in outs],
        out_shape=[o[0] for o in outs],
        scratch_shapes=[pltpu.VMEM((tm, d), BF16)],
        compiler_params=pltpu.CompilerParams(
            dimension_semantics=("arbitrary", "arbitrary"), vmem_limit_bytes=VMEM_LIMIT),
        name="inproj",
    )(x_all, g1, w1, wf, bf, qg, kg)


def _cumsum_kernel(lf_ref, crow_ref, carry_scr):
    @pl.when(pl.program_id(0) == 0)
    def _():
        carry_scr[...] = jnp.zeros_like(carry_scr)

    lf = lf_ref[...]
    t = lf.shape[0]
    r = lax.broadcasted_iota(jnp.int32, (t, t), 0)
    c = lax.broadcasted_iota(jnp.int32, (t, t), 1)
    tri = (c <= r).astype(BF16)
    hi, mid, lo = _split3(lf)
    cs = _dot(tri, hi) + _dot(tri, mid) + _dot(tri, lo) + carry_scr[...]
    carry_scr[...] = cs[t - 1:t, :]
    crow_ref[...] = (cs * LOG2E).T[:N_HEADS, :]


def _cumsum_rows(lf, *, n_rows, t):
    return pl.pallas_call(
        _cumsum_kernel,
        grid=(n_rows // t,),
        in_specs=[pl.BlockSpec((t, 128), lambda i: (i, 0))],
        out_specs=pl.BlockSpec((N_HEADS, t), lambda i: (0, i)),
        out_shape=jax.ShapeDtypeStruct((N_HEADS, n_rows), F32),
        scratch_shapes=[pltpu.VMEM((1, 128), F32)],
        compiler_params=pltpu.CompilerParams(dimension_semantics=("arbitrary",)),
        name="logf_cumsum",
    )(lf)


def _flash_kernel(q_ref, k_ref, v_ref, c_ref, zero_ref, o_ref,
                  s_scr, p_scr, m_scr, l_scr, a_scr, acc_scr, *, t, rc):
    del zero_ref
    i = pl.program_id(1)
    q = q_ref[...]
    m_scr[...] = jnp.full_like(m_scr, -jnp.inf)
    l_scr[...] = jnp.zeros_like(l_scr)
    acc_scr[...] = jnp.zeros_like(acc_scr)

    def scores(j, slot):
        k0 = pl.multiple_of(j * t, t)
        s_scr[slot] = lax.dot_general(q, k_ref[pl.ds(k0, t), :], (((1,), (1,)), ((), ())),
                                      preferred_element_type=F32)

    def weighted_values(j, slot):
        k0 = pl.multiple_of(j * t, t)
        acc_scr[...] = a_scr[slot] * acc_scr[...] + _dot(p_scr[slot], v_ref[pl.ds(k0, t), :])

    def softmax(j, slot, masked):
        cj = c_ref[j]

        def chunk(r, carry):
            rows = pl.ds(pl.multiple_of(r * rc, rc), rc)
            s = s_scr[slot, rows, :] - cj
            if masked:
                row = r * rc + lax.broadcasted_iota(jnp.int32, (rc, t), 0)
                col = lax.broadcasted_iota(jnp.int32, (rc, t), 1)
                s = jnp.where(col <= row, s, -jnp.inf)
            m_prev = m_scr[rows, :]
            m_new = jnp.maximum(m_prev, jnp.broadcast_to(jnp.max(s, axis=-1, keepdims=True), (rc, D_HEAD)))
            alpha = jnp.exp2(m_prev - m_new)
            p = jnp.exp2(s - jnp.tile(m_new, (1, t // D_HEAD)))
            l_scr[rows, :] = alpha * l_scr[rows, :] + jnp.broadcast_to(
                jnp.sum(p, axis=-1, keepdims=True), (rc, D_HEAD))
            m_scr[rows, :] = m_new
            a_scr[slot, rows, :] = alpha
            p_scr[slot, rows, :] = p.astype(BF16)
            return carry

        lax.fori_loop(0, t // rc, chunk, 0, unroll=True)

    p_scr[1] = jnp.zeros((t, t), BF16)
    a_scr[1] = jnp.ones((t, D_HEAD), F32)
    scores(0, 0)

    def pair(jj, carry):
        j = 2 * jj

        @pl.when(jj >= 0)
        def _():
            scores(j + 1, 1)
            softmax(j, 0, False)
            weighted_values(jnp.maximum(j - 1, 0), 1)

        @pl.when(jj >= 0)
        def _():
            scores(j + 2, 0)
            softmax(j + 1, 1, False)
            weighted_values(j, 0)

        return carry

    n_pairs = i // 2
    lax.fori_loop(0, n_pairs, pair, 0)
    prev = jnp.maximum(2 * n_pairs - 1, 0)

    @pl.when(i % 2 == 0)
    def _():
        softmax(i, 0, True)
        weighted_values(prev, 1)
        weighted_values(i, 0)

    @pl.when(i % 2 == 1)
    def _():
        scores(i, 1)
        softmax(i - 1, 0, False)
        weighted_values(prev, 1)

    @pl.when(i % 2 == 1)
    def _():
        softmax(i, 1, True)
        weighted_values(i - 1, 0)
        weighted_values(i, 1)

    o_ref[...] = (acc_scr[...] * (1.0 / l_scr[...])).astype(o_ref.dtype)


def _flash(q, kb, vb, c_tiles, *, t, rc):
    rows = q.shape[0]
    nk = c_tiles.shape[1]
    kernel = functools.partial(_flash_kernel, t=t, rc=rc)
    return pl.pallas_call(
        kernel,
        grid=(N_HEADS, nk),
        in_specs=[pl.BlockSpec((t, D_HEAD), lambda h, i: (i, h)),
                  pl.BlockSpec((nk * t, D_HEAD), lambda h, i: (0, h)),
                  pl.BlockSpec((nk * t, D_HEAD), lambda h, i: (0, h)),
                  pl.BlockSpec((None, nk, 1, t), lambda h, i: (h, 0, 0, 0)),
                  pl.BlockSpec(memory_space=pl.ANY)],
        out_specs=pl.BlockSpec((t, D_HEAD), lambda h, i: (i, h)),
        out_shape=jax.ShapeDtypeStruct((rows, D_ATTN), BF16),
        input_output_aliases={4: 0},
        scratch_shapes=[pltpu.VMEM((2, t, t), F32), pltpu.VMEM((2, t, t), BF16),
                        pltpu.VMEM((t, D_HEAD), F32), pltpu.VMEM((t, D_HEAD), F32),
                        pltpu.VMEM((2, t, D_HEAD), F32), pltpu.VMEM((t, D_HEAD), F32)],
        compiler_params=pltpu.CompilerParams(
            dimension_semantics=("arbitrary", "arbitrary"), vmem_limit_bytes=VMEM_LIMIT),
        name="fox_prompt",
    )(q, kb, vb, c_tiles, jnp.zeros((rows, D_ATTN), BF16))


def _decode_kernel(pt_ref, q_ref, kc_ref, vc_ref, lfc_ref, *refs, pp, group):
    del pt_ref
    k_refs = refs[:pp]
    v_refs = refs[pp:2 * pp]
    lf_refs = refs[2 * pp:3 * pp]
    o_ref = refs[3 * pp]
    m_scr, l_scr, acc_scr, carry_scr, wide_scr = refs[3 * pp + 1:]
    g = pl.program_id(1)
    n_g = pl.num_programs(1)
    page = lf_refs[0].shape[-1]

    q = q_ref[0]
    lane = lax.broadcasted_iota(jnp.int32, (N_HEADS, 128), 1)
    sub = lax.broadcasted_iota(jnp.int32, (N_HEADS, 128), 0)
    diag = lane == sub
    later = (lax.broadcasted_iota(jnp.int32, (page, page), 0)
             > lax.broadcasted_iota(jnp.int32, (page, page), 1)).astype(F32)

    @pl.when(g == 0)
    def _():
        m_scr[...] = jnp.full_like(m_scr, -jnp.inf)
        l_scr[...] = jnp.zeros_like(l_scr)
        acc_scr[...] = jnp.zeros_like(acc_scr)
        cur = jnp.where(diag, jnp.broadcast_to(lfc_ref[0], (N_HEADS, 128)), 0.0)
        carry_scr[...] = jnp.broadcast_to(jnp.sum(cur, axis=-1, keepdims=True) * LOG2E, (N_HEADS, 128))

    m_run = m_scr[...]
    l_run = l_scr[...]
    acc_run = acc_scr[...]
    for pi in range(pp):
        x = lf_refs[pi][...] * LOG2E
        hi, mid, lo = _split3(x)
        carry = carry_scr[...]
        bias = (_dot(hi.astype(F32), later) + _dot(mid.astype(F32), later)
                + _dot(lo.astype(F32), later) + carry)
        carry_scr[...] = carry + jnp.sum(x, axis=-1, keepdims=True)
        wide_scr[pi] = jnp.concatenate([bias, jnp.zeros((page - N_HEADS, page), F32)], axis=0).T
        k_ref, v_ref = k_refs[pi], v_refs[pi]
        for t0 in range(0, page, group):
            ss = []
            for t in range(t0, t0 + group):
                b_t = jnp.where(diag, jnp.broadcast_to(wide_scr[pi, t:t + 1, :], (N_HEADS, 128)), 0.0)
                s_t = jnp.sum(k_ref[t] * q + b_t, axis=-1, keepdims=True)
                ss.append(jnp.broadcast_to(s_t, (N_HEADS, 128)))
            m_new = jnp.maximum(m_run, _tree(jnp.maximum, ss))
            alpha = jnp.exp2(m_run - m_new)
            ps = [jnp.exp2(s_t - m_new) for s_t in ss]
            l_run = alpha * l_run + _tree(jnp.add, ps)
            acc_run = alpha * acc_run + _tree(jnp.add, [p_t * v_ref[t0 + n] for n, p_t in enumerate(ps)])
            m_run = m_new

    m_scr[...] = m_run
    l_scr[...] = l_run
    acc_scr[...] = acc_run

    @pl.when(g == n_g - 1)
    def _():
        s_cur = jnp.sum(kc_ref[0] * q, axis=-1, keepdims=True)
        m_new = jnp.maximum(m_run, s_cur)
        alpha = jnp.exp2(m_run - m_new)
        p = jnp.exp2(s_cur - m_new)
        l = alpha * l_run + p
        acc = alpha * acc_run + p * vc_ref[0]
        o_ref[0] = acc * (1.0 / l)


def _decode(page_table, q_s, k_cur, v_cur, lf_cur, cache_k, cache_v, cache_lft, *, pp, group):
    b, n_pages = page_table.shape
    page = cache_lft.shape[-1]
    n_g = n_pages // pp
    kernel = functools.partial(_decode_kernel, pp=pp, group=group)
    seq_blk = lambda w: pl.BlockSpec((1, w, 128), lambda s, g, pt: (s, 0, 0))

    def page_map(pi, nd):
        def index_map(s, g, pt):
            return (pt[s * n_pages + (n_pages - 1 - (g * pp + pi))],) + (0,) * nd
        return index_map

    k_specs = [pl.BlockSpec((None, page, N_HEADS, D_HEAD), page_map(pi, 3)) for pi in range(pp)]
    v_specs = [pl.BlockSpec((None, page, N_HEADS, D_HEAD), page_map(pi, 3)) for pi in range(pp)]
    lf_specs = [pl.BlockSpec((None, N_HEADS, page), page_map(pi, 2)) for pi in range(pp)]
    grid_spec = pltpu.PrefetchScalarGridSpec(
        num_scalar_prefetch=1,
        grid=(b, n_g),
        in_specs=[seq_blk(N_HEADS), seq_blk(N_HEADS), seq_blk(N_HEADS), seq_blk(1)]
                 + k_specs + v_specs + lf_specs,
        out_specs=pl.BlockSpec((1, N_HEADS, D_HEAD), lambda s, g, pt: (s, 0, 0)),
        scratch_shapes=[pltpu.VMEM((N_HEADS, 128), F32), pltpu.VMEM((N_HEADS, 128), F32),
                        pltpu.VMEM((N_HEADS, D_HEAD), F32), pltpu.VMEM((N_HEADS, 128), F32),
                        pltpu.VMEM((pp, page, page), F32)],
    )
    return pl.pallas_call(
        kernel,
        grid_spec=grid_spec,
        out_shape=jax.ShapeDtypeStruct((b, N_HEADS, D_HEAD), F32),
        compiler_params=pltpu.CompilerParams(
            dimension_semantics=("arbitrary", "arbitrary"), vmem_limit_bytes=VMEM_LIMIT),
        name="fox_decode",
    )(page_table.reshape(-1), q_s, k_cur, v_cur, lf_cur,
      *([cache_k] * pp), *([cache_v] * pp), *([cache_lft] * pp))


def _gelu_tanh(y):
    return 0.5 * y * (1.0 + jnp.tanh(math.sqrt(2.0 / math.pi) * (y + 0.044715 * (y * y * y))))


def _ssm_kernel(u_ref, bre_ref, bim_ref, cre_ref, cim_ref,
                are_ref, aim_ref, akre_ref, akim_ref, d_ref, wg_ref, bg_ref, h0re_ref, h0im_ref,
                o_ref, hpre_ref, hpim_ref, hsre_ref, hsim_ref,
                sre_scr, sim_scr, endre_scr, endim_scr, hinre_scr, hinim_scr, cre_scr, cim_scr,
                *, n_prompt_chunks, steps, slabs_per_pass, n_sample):
    c = pl.program_id(0)
    n_slab = sre_scr.shape[0]
    n_blk = bre_ref.shape[0]
    blk_slabs = n_slab // n_blk
    lanes = lambda sb: slice(sb * 128, (sb + 1) * 128)

    def input_proj(ub):
        for j in range(n_blk):
            uj = ub[:, j * 128:(j + 1) * 128]
            zr = _dot(uj, bre_ref[j])
            zi = _dot(uj, bim_ref[j])
            for i in range(blk_slabs):
                sre_scr[j * blk_slabs + i] = zr[:, lanes(i)]
                sim_scr[j * blk_slabs + i] = zi[:, lanes(i)]

    def tail(uf):
        ys = []
        for j in range(n_blk):
            sbs = range(j * blk_slabs, (j + 1) * blk_slabs)
            hr = jnp.concatenate([sre_scr[sb] for sb in sbs], axis=-1).astype(BF16)
            hi = jnp.concatenate([sim_scr[sb] for sb in sbs], axis=-1).astype(BF16)
            ys.append(_dot(hr, cre_ref[j]) + _dot(hi, cim_ref[j]))
        y = jnp.concatenate(ys, axis=-1) + d_ref[...] * uf
        gl = _gelu_tanh(y)
        return gl * jax.nn.sigmoid(_dot(gl.astype(BF16), wg_ref[...]) + bg_ref[...])

    def scan_pass(store):
        for g0 in range(0, n_slab, slabs_per_pass):
            sbs = list(range(g0, g0 + slabs_per_pass))
            ar = [jnp.broadcast_to(are_ref[:, lanes(sb)], (8, 128)) for sb in sbs]
            ai = [jnp.broadcast_to(aim_ref[:, lanes(sb)], (8, 128)) for sb in sbs]
            if store:
                init = ([hinre_scr[:, lanes(sb)] for sb in sbs], [hinim_scr[:, lanes(sb)] for sb in sbs])
            else:
                zero = jnp.zeros((8, 128), F32)
                init = ([zero] * len(sbs), [zero] * len(sbs))

            def body(k, carry, sbs=sbs, ar=ar, ai=ai):
                re, im = carry
                rows = pl.ds(k, 8, stride=steps)
                nre, nim = [], []
                for n, sb in enumerate(sbs):
                    r = ar[n] * re[n] - ai[n] * im[n] + sre_scr[sb, rows, :]
                    i = ar[n] * im[n] + ai[n] * re[n] + sim_scr[sb, rows, :]
                    if store:
                        sre_scr[sb, rows, :] = r
                        sim_scr[sb, rows, :] = i
                    nre.append(r)
                    nim.append(i)
                return nre, nim

            re, im = lax.fori_loop(0, steps, body, init, unroll=2)
            if not store:
                for n, sb in enumerate(sbs):
                    endre_scr[:, lanes(sb)] = re[n]
                    endim_scr[:, lanes(sb)] = im[n]

    @pl.when(c == 0)
    def _():
        cre_scr[...] = jnp.zeros_like(cre_scr)
        cim_scr[...] = jnp.zeros_like(cim_scr)

    @pl.when(c < n_prompt_chunks)
    def _():
        uf = u_ref[...]
        input_proj(uf.astype(BF16))
        scan_pass(False)
        cr = cre_scr[...]
        ci = cim_scr[...]
        akr = akre_ref[...]
        aki = akim_ref[...]
        for s in range(8):
            hinre_scr[s:s + 1, :] = cr
            hinim_scr[s:s + 1, :] = ci
            er = endre_scr[s:s + 1, :]
            ei = endim_scr[s:s + 1, :]
            cr, ci = er + (akr * cr - aki * ci), ei + (akr * ci + aki * cr)
        cre_scr[...] = cr
        cim_scr[...] = ci
        hpre_ref[...] = cr
        hpim_ref[...] = ci
        scan_pass(True)
        o_ref[...] = tail(uf).astype(o_ref.dtype)

    @pl.when(c == n_prompt_chunks)
    def _():
        uf = u_ref[...]
        input_proj(uf.astype(BF16))
        rs = slice(SAMPLE_OFF, SAMPLE_OFF + n_sample)
        for sb in range(n_slab):
            ar = are_ref[:, lanes(sb)]
            ai = aim_ref[:, lanes(sb)]
            h0r = h0re_ref[:, lanes(sb)]
            h0i = h0im_ref[:, lanes(sb)]
            nre = ar * h0r - ai * h0i + sre_scr[sb, rs, :]
            nim = ar * h0i + ai * h0r + sim_scr[sb, rs, :]
            sre_scr[sb, rs, :] = nre
            sim_scr[sb, rs, :] = nim
            hsre_ref[:, lanes(sb)] = nre
            hsim_ref[:, lanes(sb)] = nim
        o_ref[...] = tail(uf).astype(o_ref.dtype)


def _ssm(u, bre, bim, cre, cim, are, aim, akre, akim, d_skip, w_glu, b_glu, h0re, h0im,
         *, t, n_prompt_chunks, slabs_per_pass):
    rows = u.shape[0]
    n_state = are.shape[1]
    n_sample = h0re.shape[0]
    full = lambda a: pl.BlockSpec(a.shape, lambda c: (0,) * a.ndim)
    kernel = functools.partial(_ssm_kernel, n_prompt_chunks=n_prompt_chunks, steps=t // 8,
                               slabs_per_pass=slabs_per_pass, n_sample=n_sample)
    state_row = jax.ShapeDtypeStruct((1, n_state), F32)
    state_smp = jax.ShapeDtypeStruct((n_sample, n_state), F32)
    consts = (bre, bim, cre, cim, are, aim, akre, akim, d_skip, w_glu, b_glu, h0re, h0im)
    return pl.pallas_call(
        kernel,
        grid=(n_prompt_chunks + 1,),
        in_specs=[pl.BlockSpec((t, u.shape[1]), lambda c: (c, 0))] + [full(a) for a in consts],
        out_specs=[pl.BlockSpec((t, u.shape[1]), lambda c: (c, 0)),
                   full(state_row), full(state_row), full(state_smp), full(state_smp)],
        out_shape=[jax.ShapeDtypeStruct((rows, u.shape[1]), BF16),
                   state_row, state_row, state_smp, state_smp],
        scratch_shapes=[pltpu.VMEM((n_state // 128, t, 128), F32), pltpu.VMEM((n_state // 128, t, 128), F32),
                        pltpu.VMEM((8, n_state), F32), pltpu.VMEM((8, n_state), F32),
                        pltpu.VMEM((8, n_state), F32), pltpu.VMEM((8, n_state), F32),
                        pltpu.VMEM((1, n_state), F32), pltpu.VMEM((1, n_state), F32)],
        compiler_params=pltpu.CompilerParams(
            dimension_semantics=("arbitrary",), vmem_limit_bytes=VMEM_LIMIT),
        name="s5_branch",
    )(u, *consts)


def _merge_kernel(attn_ref, as_ref, ssm_ref, sga_ref, sgs_ref, x_ref, wba_ref, wbs_ref, wo_ref, g2_ref,
                  x1_ref, h2_ref, ma_scr, *, sample_tile):
    n_sample = as_ref.shape[0]
    ma_scr[...] = _dot(attn_ref[...], wba_ref[...])

    @pl.when(pl.program_id(0) == sample_tile)
    def _():
        acc = jnp.zeros((n_sample, wba_ref.shape[1]), F32)
        for h in range(N_HEADS):
            acc = acc + _dot(as_ref[:, h, :].astype(BF16), wba_ref[h * D_HEAD:(h + 1) * D_HEAD, :])
        ma_scr[SAMPLE_OFF:SAMPLE_OFF + n_sample, :] = acc

    m = (sga_ref[...].astype(F32) * ma_scr[...]
         + sgs_ref[...].astype(F32) * _dot(ssm_ref[...], wbs_ref[...]))
    x1 = x_ref[...] + _dot(m.astype(BF16), wo_ref[...])
    x1_ref[...] = x1
    ms = jnp.mean(x1 * x1, axis=-1, keepdims=True)
    h2_ref[...] = ((x1 * lax.rsqrt(ms + EPS)) * g2_ref[...]).astype(BF16)


def _merge(attn, attn_s, ssm, sga, sgs, x_all, wba, wbs, wo, g2, *, tm, sample_tile):
    rows, d = x_all.shape
    row_blk = lambda w: pl.BlockSpec((tm, w), lambda i: (i, 0))
    const = lambda a: pl.BlockSpec(a.shape, lambda i: (0,) * a.ndim, pipeline_mode=pl.Buffered(1))
    kernel = functools.partial(_merge_kernel, sample_tile=sample_tile)
    return pl.pallas_call(
        kernel,
        grid=(rows // tm,),
        in_specs=[row_blk(D_ATTN), const(attn_s), row_blk(ssm.shape[1]), row_blk(d), row_blk(d), row_blk(d),
                  const(wba), const(wbs), const(wo), const(g2)],
        out_specs=[row_blk(d), row_blk(d)],
        out_shape=[jax.ShapeDtypeStruct((rows, d), F32), jax.ShapeDtypeStruct((rows, d), BF16)],
        scratch_shapes=[pltpu.VMEM((tm, d), F32)],
        compiler_params=pltpu.CompilerParams(
            dimension_semantics=("arbitrary",), vmem_limit_bytes=VMEM_LIMIT),
        name="merge",
    )(attn, attn_s, ssm, sga, sgs, x_all, wba, wbs, wo, g2)


def _ffn_kernel(h2_ref, x1_ref, w1_ref, w3_ref, w2_ref, o_ref):
    @pl.when(pl.program_id(1) == 0)
    def _():
        o_ref[...] = x1_ref[...]

    h2 = h2_ref[...]
    a = _dot(h2, w1_ref[...])
    b = _dot(h2, w3_ref[...])
    o_ref[...] += _dot(((a * jax.nn.sigmoid(a)) * b).astype(BF16), w2_ref[...])


def _ffn(h2, x1, w1, w3, w2, *, tm, tf):
    rows, d = x1.shape
    d_ff = w1.shape[1]
    return pl.pallas_call(
        _ffn_kernel,
        grid=(rows // tm, d_ff // tf),
        in_specs=[pl.BlockSpec((tm, d), lambda i, f: (i, 0)),
                  pl.BlockSpec((tm, d), lambda i, f: (i, 0)),
                  pl.BlockSpec((d, tf), lambda i, f: (0, f)),
                  pl.BlockSpec((d, tf), lambda i, f: (0, f)),
                  pl.BlockSpec((tf, d), lambda i, f: (f, 0))],
        out_specs=pl.BlockSpec((tm, d), lambda i, f: (i, 0)),
        out_shape=jax.ShapeDtypeStruct((rows, d), F32),
        compiler_params=pltpu.CompilerParams(
            dimension_semantics=("arbitrary", "arbitrary"), vmem_limit_bytes=VMEM_LIMIT),
        name="ffn",
    )(h2, x1, w1, w3, w2)


def _ssm_tables(lam_re, lam_im, log_dt, b_re, b_im, c_re, c_im, steps):
    n_groups = lam_re.shape[0]
    lr, li = lam_re.astype(F32), lam_im.astype(F32)
    dt = jnp.exp(log_dt.astype(F32))[:, None]
    mag = jnp.exp(lr * dt)
    ar, ai = mag * jnp.cos(li * dt), mag * jnp.sin(li * dt)
    den = lr * lr + li * li
    cr = ((ar - 1.0) * lr + ai * li) / den
    ci = (ai * lr - (ar - 1.0) * li) / den
    br, bi = b_re.astype(F32), b_im.astype(F32)
    bbr = cr[..., None] * br - ci[..., None] * bi
    bbi = cr[..., None] * bi + ci[..., None] * br
    akr, aki = ar, ai
    for _ in range(steps - 1):
        akr, aki = akr * ar - aki * ai, akr * ai + aki * ar
    n_blk = n_groups // GROUPS_PER_BLOCK
    eye = jnp.eye(GROUPS_PER_BLOCK, dtype=F32)

    def in_blocks(m):
        m = m.reshape(n_blk, GROUPS_PER_BLOCK, P_STATE, SSM_GROUP)
        return jnp.einsum('jgpc,gh->jgchp', m, eye).reshape(
            n_blk, GROUPS_PER_BLOCK * SSM_GROUP, GROUPS_PER_BLOCK * P_STATE).astype(BF16)

    def out_blocks(m):
        m = m.reshape(n_blk, GROUPS_PER_BLOCK, SSM_GROUP, P_STATE)
        return jnp.einsum('jgcp,gh->jgphc', m, eye).reshape(
            n_blk, GROUPS_PER_BLOCK * P_STATE, GROUPS_PER_BLOCK * SSM_GROUP).astype(BF16)

    flat = lambda a: a.reshape(1, -1)
    return (in_blocks(bbr), in_blocks(bbi),
            out_blocks(c_re.astype(F32)), out_blocks(-c_im.astype(F32)),
            flat(ar), flat(ai), flat(akr), flat(aki))


def _tiles(l_prompt):
    if l_prompt == 8208:
        return dict(t_ssm=432, t_attn=640, rc=32, tm_ffn=864, tf=512, slabs_per_pass=4, pp=16, group=32)
    return dict(t_ssm=200, t_attn=128, rc=32, tm_ffn=300, tf=512, slabs_per_pass=4, pp=2, group=32)


def kernel(x_prompt, x_sample, cache_k, cache_v, cache_logf, state_ssm, page_table, meta, norm1_g, w_in, b_f, q_norm_g, k_norm_g, lam_re, lam_im, log_dt, b_re, b_im, c_re, c_im, d_skip, w_glu, b_glu, w_br_attn, w_br_ssm, w_out, norm2_g, w_ff1, w_ff3, w_ff2):
    assert x_prompt.shape[0] == 1 and x_sample.shape[1] == 1 and w_in.shape[0] == 1
    n_meta, d = meta.shape
    seq = x_prompt.shape[1]
    n_sample = x_sample.shape[0]
    l_prompt = n_meta + seq
    cfg = _tiles(l_prompt)
    t_ssm = cfg["t_ssm"]
    assert l_prompt % t_ssm == 0 and SAMPLE_OFF + n_sample <= t_ssm
    n_p = l_prompt // t_ssm
    rows = l_prompt + t_ssm
    s0 = l_prompt + SAMPLE_OFF
    d_ssm = w_glu.shape[1]
    n_groups = d_ssm // SSM_GROUP

    x_all = jnp.concatenate([
        meta.astype(F32), x_prompt[0], jnp.zeros((SAMPLE_OFF, d), F32), x_sample[:, 0],
        jnp.zeros((rows - s0 - n_sample, d), F32)], axis=0)
    w = w_in[0]
    c0, c1 = 3 * D_ATTN, 3 * D_ATTN + N_HEADS
    n_w1 = w.shape[1] - N_HEADS
    col = lax.broadcasted_iota(jnp.int32, (1, n_w1), 1)
    w1 = jnp.where(col < c0, w[:, :n_w1], w[:, N_HEADS:]).astype(BF16)
    wf = jnp.pad(w[:, c0:c1], ((0, 0), (0, 128 - N_HEADS))).astype(BF16)
    bf = jnp.pad(b_f[0].astype(F32), (0, 128 - N_HEADS)).reshape(1, 128)
    row = lambda a: a.astype(F32).reshape(1, -1)

    q, kf, ks, kb, vf, vs, vb, u, sga, sgs, lf = _inproj(
        x_all, row(norm1_g[0]), w1, wf, bf, row(q_norm_g[0]), row(k_norm_g[0]),
        tm=t_ssm, l_prompt=l_prompt, n_sample=n_sample)

    ta = cfg["t_attn"]
    nk = -(-l_prompt // ta)
    assert nk * ta <= rows
    c_rows = _cumsum_rows(lf, n_rows=nk * ta, t=ta)
    attn = _flash(q, kb, vb, c_rows.reshape(N_HEADS, nk, 1, ta), t=ta, rc=cfg["rc"])

    smp = slice(s0, s0 + n_sample)
    q_s = q[smp].astype(F32).reshape(n_sample, N_HEADS, D_HEAD)
    lf_cur = lf[smp].reshape(n_sample, 1, 128)
    attn_s = _decode(page_table, q_s, ks, vs, lf_cur, cache_k[0], cache_v[0],
                     jnp.swapaxes(cache_logf[0], 1, 2), pp=cfg["pp"], group=cfg["group"])

    steps = t_ssm // 8
    bre, bim, cre, cim, are, aim, akre, akim = _ssm_tables(
        lam_re[0], lam_im[0], log_dt[0], b_re[0], b_im[0], c_re[0], c_im[0], steps)
    h0 = state_ssm[0].astype(F32).reshape(n_sample, n_groups * P_STATE, 2)
    ssm, hpre, hpim, hsre, hsim = _ssm(
        u, bre, bim, cre, cim, are, aim, akre, akim,
        row(d_skip[0]), w_glu[0].astype(BF16), row(b_glu[0]), h0[..., 0], h0[..., 1],
        t=t_ssm, n_prompt_chunks=n_p, slabs_per_pass=cfg["slabs_per_pass"])

    x1, h2 = _merge(attn, attn_s, ssm, sga, sgs, x_all, w_br_attn[0].astype(BF16),
                    w_br_ssm[0].astype(BF16), w_out[0].astype(BF16), row(norm2_g[0]),
                    tm=t_ssm, sample_tile=n_p)
    y = _ffn(h2, x1, w_ff1[0].astype(BF16), w_ff3[0].astype(BF16), w_ff2[0].astype(BF16),
             tm=cfg["tm_ffn"], tf=cfg["tf"])

    state = lambda re, im, n: jnp.stack([re, im], axis=-1).reshape(1, n, n_groups, P_STATE, 2)
    return (y[n_meta:l_prompt][None],
            y[smp][:, None],
            kf.reshape(1, 1, l_prompt, N_HEADS, D_HEAD),
            vf.reshape(1, 1, l_prompt, N_HEADS, D_HEAD),
            lf[:l_prompt, :N_HEADS].reshape(1, 1, l_prompt, N_HEADS),
            state(hpre, hpim, 1),
            ks.reshape(1, n_sample, 1, N_HEADS, D_HEAD),
            vs.reshape(1, n_sample, 1, N_HEADS, D_HEAD),
            lf[smp, :N_HEADS].reshape(1, n_sample, 1, N_HEADS),
            state(hsre, hsim, n_sample))
```

```python
import functools
import math

import jax
import jax.numpy as jnp
from jax import lax
from jax.experimental import pallas as pl
from jax.experimental.pallas import tpu as pltpu

F32 = jnp.float32
BF16 = jnp.bfloat16

N_HEADS = 8
D_HEAD = 128
D_ATTN = N_HEADS * D_HEAD
SSM_GROUP = 16
P_STATE = 64
EPS = 1e-6
SAMPLE_OFF = 16
GROUPS_PER_BLOCK = 8
VMEM_LIMIT = 56 * 1024 * 1024
LOG2E = math.log2(math.e)


def _dot(a, b):
    return jnp.dot(a, b, preferred_element_type=F32)


def _split3(x):
    hi = x.astype(BF16)
    r1 = x - hi.astype(F32)
    mid = r1.astype(BF16)
    lo = (r1 - mid.astype(F32)).astype(BF16)
    return hi, mid, lo


def _tree(op, xs):
    xs = list(xs)
    while len(xs) > 1:
        xs = [op(xs[i], xs[i + 1]) for i in range(0, len(xs) - 1, 2)] + ([xs[-1]] if len(xs) % 2 else [])
    return xs[0]


def _log_sigmoid(f):
    return jnp.minimum(f, 0.0) - jnp.log1p(jnp.exp(-jnp.abs(f)))


def _head_rmsnorm(z, g):
    outs = []
    for h in range(N_HEADS):
        zh = z[:, h * D_HEAD:(h + 1) * D_HEAD]
        ms = jnp.broadcast_to(jnp.mean(zh * zh, axis=-1, keepdims=True), zh.shape)
        outs.append(zh * lax.rsqrt(ms + EPS) * g)
    return jnp.concatenate(outs, axis=-1)


def _inproj_kernel(x_ref, g1_ref, w_ref, wf_ref, bf_ref, qg_ref, kg_ref,
                   q_ref, kf_ref, ks_ref, kb_ref, vf_ref, vs_ref, vb_ref, u_ref, sga_ref, sgs_ref, lf_ref,
                   h_scr, *, n_prompt_tiles, n_sample):
    i = pl.program_id(0)
    j = pl.program_id(1)
    smp = slice(SAMPLE_OFF, SAMPLE_OFF + n_sample)

    @pl.when(j == 0)
    def _():
        x = x_ref[...]
        ms = jnp.mean(x * x, axis=-1, keepdims=True)
        hb = ((x * lax.rsqrt(ms + EPS)) * g1_ref[...]).astype(BF16)
        h_scr[...] = hb
        lf_ref[...] = _log_sigmoid(_dot(hb, wf_ref[...]) + bf_ref[...])

    def proj():
        return _dot(h_scr[...], w_ref[...])

    def store_heads(val, full_ref, smp_ref):
        @pl.when(i < n_prompt_tiles)
        def _():
            full_ref[...] = val

        @pl.when(i == n_prompt_tiles)
        def _():
            for h in range(N_HEADS):
                smp_ref[:, h, :] = val[smp, h * D_HEAD:(h + 1) * D_HEAD]

    @pl.when(j == 0)
    def _():
        q_ref[...] = (_head_rmsnorm(proj(), qg_ref[...]) * (D_HEAD ** -0.5 * LOG2E)).astype(BF16)

    @pl.when(j == 1)
    def _():
        kn = _head_rmsnorm(proj(), kg_ref[...])
        kb_ref[...] = kn.astype(BF16)
        store_heads(kn, kf_ref, ks_ref)

    @pl.when(j == 2)
    def _():
        z = proj()
        vb_ref[...] = z.astype(BF16)
        store_heads(z, vf_ref, vs_ref)

    @pl.when(j == 3)
    def _():
        u_ref[...] = proj()

    for jj, ref, half in ((4, sga_ref, 0), (5, sga_ref, 1), (6, sgs_ref, 0), (7, sgs_ref, 1)):
        @pl.when(j == jj)
        def _(ref=ref, half=half):
            ref[:, half * 1024:(half + 1) * 1024] = jax.nn.sigmoid(proj()).astype(BF16)


def _inproj(x_all, g1, w1, wf, bf, qg, kg, *, tm, l_prompt, n_sample):
    rows, d = x_all.shape
    n_col = w1.shape[1] // 1024
    n_p = l_prompt // tm
    row_blk = lambda w: pl.BlockSpec((tm, w), lambda i, j: (i, 0))
    full = lambda a: pl.BlockSpec(a.shape, lambda i, j: (0,) * a.ndim)
    heads_p = pl.BlockSpec((tm, D_ATTN), lambda i, j: (jnp.minimum(i, n_p - 1), 0))
    heads_s = pl.BlockSpec((n_sample, N_HEADS, D_HEAD), lambda i, j: (0, 0, 0))
    f32_p = jax.ShapeDtypeStruct((l_prompt, D_ATTN), F32)
    f32_s = jax.ShapeDtypeStruct((n_sample, N_HEADS, D_HEAD), F32)
    outs = [
        (jax.ShapeDtypeStruct((rows, D_ATTN), BF16), row_blk(D_ATTN)),
        (f32_p, heads_p), (f32_s, heads_s),
        (jax.ShapeDtypeStruct((rows, D_ATTN), BF16), row_blk(D_ATTN)),
        (f32_p, heads_p), (f32_s, heads_s),
        (jax.ShapeDtypeStruct((rows, D_ATTN), BF16), row_blk(D_ATTN)),
        (jax.ShapeDtypeStruct((rows, 1024), F32), row_blk(1024)),
        (jax.ShapeDtypeStruct((rows, d), BF16), row_blk(d)),
        (jax.ShapeDtypeStruct((rows, d), BF16), row_blk(d)),
        (jax.ShapeDtypeStruct((rows, 128), F32), row_blk(128)),
    ]
    kernel = functools.partial(_inproj_kernel, n_prompt_tiles=n_p, n_sample=n_sample)
    return pl.pallas_call(
        kernel,
        grid=(rows // tm, n_col),
        in_specs=[row_blk(d), full(g1), pl.BlockSpec((d, 1024), lambda i, j: (0, j)),
                  full(wf), full(bf), full(qg), full(kg)],
        out_specs=[o[1] for o in outs],
        out_shape=[o[0] for o in outs],
        scratch_shapes=[pltpu.VMEM((tm, d), BF16)],
        compiler_params=pltpu.CompilerParams(
            dimension_semantics=("arbitrary", "arbitrary"), vmem_limit_bytes=VMEM_LIMIT),
        name="inproj",
    )(x_all, g1, w1, wf, bf, qg, kg)


def _cumsum_kernel(lf_ref, crow_ref, carry_scr):
    @pl.when(pl.program_id(0) == 0)
    def _():
        carry_scr[...] = jnp.zeros_like(carry_scr)

    lf = lf_ref[...]
    t = lf.shape[0]
    r = lax.broadcasted_iota(jnp.int32, (t, t), 0)
    c = lax.broadcasted_iota(jnp.int32, (t, t), 1)
    tri = (c <= r).astype(BF16)
    hi, mid, lo = _split3(lf)
    cs = _dot(tri, hi) + _dot(tri, mid) + _dot(tri, lo) + carry_scr[...]
    carry_scr[...] = cs[t - 1:t, :]
    crow_ref[...] = (cs * LOG2E).T[:N_HEADS, :]


def _cumsum_rows(lf, *, n_rows, t):
    return pl.pallas_call(
        _cumsum_kernel,
        grid=(n_rows // t,),
        in_specs=[pl.BlockSpec((t, 128), lambda i: (i, 0))],
        out_specs=pl.BlockSpec((N_HEADS, t), lambda i: (0, i)),
        out_shape=jax.ShapeDtypeStruct((N_HEADS, n_rows), F32),
        scratch_shapes=[pltpu.VMEM((1, 128), F32)],
        compiler_params=pltpu.CompilerParams(dimension_semantics=("arbitrary",)),
        name="logf_cumsum",
    )(lf)


def _flash_kernel(q_ref, k_ref, v_ref, c_ref, zero_ref, o_ref,
                  s_scr, p_scr, m_scr, l_scr, a_scr, acc_scr, *, t, rc):
    del zero_ref
    i = pl.program_id(1)
    q = q_ref[...]
    m_scr[...] = jnp.full_like(m_scr, -jnp.inf)
    l_scr[...] = jnp.zeros_like(l_scr)
    acc_scr[...] = jnp.zeros_like(acc_scr)

    def scores(j, slot):
        k0 = pl.multiple_of(j * t, t)
        s_scr[slot] = lax.dot_general(q, k_ref[pl.ds(k0, t), :], (((1,), (1,)), ((), ())),
                                      preferred_element_type=F32)

    def weighted_values(j, slot):
        k0 = pl.multiple_of(j * t, t)
        acc_scr[...] = a_scr[slot] * acc_scr[...] + _dot(p_scr[slot], v_ref[pl.ds(k0, t), :])

    def softmax(j, slot, masked):
        cj = c_ref[j]

        def chunk(r, carry):
            rows = pl.ds(pl.multiple_of(r * rc, rc), rc)
            s = s_scr[slot, rows, :] - cj
            if masked:
                row = r * rc + lax.broadcasted_iota(jnp.int32, (rc, t), 0)
                col = lax.broadcasted_iota(jnp.int32, (rc, t), 1)
                s = jnp.where(col <= row, s, -jnp.inf)
            m_prev = m_scr[rows, :]
            m_new = jnp.maximum(m_prev, jnp.broadcast_to(jnp.max(s, axis=-1, keepdims=True), (rc, D_HEAD)))
            alpha = jnp.exp2(m_prev - m_new)
            p = jnp.exp2(s - jnp.tile(m_new, (1, t // D_HEAD)))
            l_scr[rows, :] = alpha * l_scr[rows, :] + jnp.broadcast_to(
                jnp.sum(p, axis=-1, keepdims=True), (rc, D_HEAD))
            m_scr[rows, :] = m_new
            a_scr[slot, rows, :] = alpha
            p_scr[slot, rows, :] = p.astype(BF16)
            return carry

        lax.fori_loop(0, t // rc, chunk, 0, unroll=True)

    p_scr[1] = jnp.zeros((t, t), BF16)
    a_scr[1] = jnp.ones((t, D_HEAD), F32)
    scores(0, 0)

    def pair(jj, carry):
        j = 2 * jj

        @pl.when(jj >= 0)
        def _():
            scores(j + 1, 1)
            softmax(j, 0, False)
            weighted_values(jnp.maximum(j - 1, 0), 1)

        @pl.when(jj >= 0)
        def _():
            scores(j + 2, 0)
            softmax(j + 1, 1, False)
            weighted_values(j, 0)

        return carry

    n_pairs = i // 2
    lax.fori_loop(0, n_pairs, pair, 0)
    prev = jnp.maximum(2 * n_pairs - 1, 0)

    @pl.when(i % 2 == 0)
    def _():
        softmax(i, 0, True)
        weighted_values(prev, 1)
        weighted_values(i, 0)

    @pl.when(i % 2 == 1)
    def _():
        scores(i, 1)
        softmax(i - 1, 0, False)
        weighted_values(prev, 1)

    @pl.when(i % 2 == 1)
    def _():
        softmax(i, 1, True)
        weighted_values(i - 1, 0)
        weighted_values(i, 1)

    o_ref[...] = (acc_scr[...] * (1.0 / l_scr[...])).astype(o_ref.dtype)


def _flash(q, kb, vb, c_tiles, *, t, rc):
    rows = q.shape[0]
    nk = c_tiles.shape[1]
    kernel = functools.partial(_flash_kernel, t=t, rc=rc)
    return pl.pallas_call(
        kernel,
        grid=(N_HEADS, nk),
        in_specs=[pl.BlockSpec((t, D_HEAD), lambda h, i: (i, h)),
                  pl.BlockSpec((nk * t, D_HEAD), lambda h, i: (0, h)),
                  pl.BlockSpec((nk * t, D_HEAD), lambda h, i: (0, h)),
                  pl.BlockSpec((None, nk, 1, t), lambda h, i: (h, 0, 0, 0)),
                  pl.BlockSpec(memory_space=pl.ANY)],
        out_specs=pl.BlockSpec((t, D_HEAD), lambda h, i: (i, h)),
        out_shape=jax.ShapeDtypeStruct((rows, D_ATTN), BF16),
        input_output_aliases={4: 0},
        scratch_shapes=[pltpu.VMEM((2, t, t), F32), pltpu.VMEM((2, t, t), BF16),
                        pltpu.VMEM((t, D_HEAD), F32), pltpu.VMEM((t, D_HEAD), F32),
                        pltpu.VMEM((2, t, D_HEAD), F32), pltpu.VMEM((t, D_HEAD), F32)],
        compiler_params=pltpu.CompilerParams(
            dimension_semantics=("arbitrary", "arbitrary"), vmem_limit_bytes=VMEM_LIMIT),
        name="fox_prompt",
    )(q, kb, vb, c_tiles, jnp.zeros((rows, D_ATTN), BF16))


def _decode_kernel(pt_ref, q_ref, kc_ref, vc_ref, lfc_ref, *refs, pp, group):
    del pt_ref
    k_refs = refs[:pp]
    v_refs = refs[pp:2 * pp]
    lf_refs = refs[2 * pp:3 * pp]
    o_ref = refs[3 * pp]
    m_scr, l_scr, acc_scr, carry_scr, wide_scr = refs[3 * pp + 1:]
    g = pl.program_id(1)
    n_g = pl.num_programs(1)
    page = lf_refs[0].shape[-1]

    q = q_ref[0]
    lane = lax.broadcasted_iota(jnp.int32, (N_HEADS, 128), 1)
    sub = lax.broadcasted_iota(jnp.int32, (N_HEADS, 128), 0)
    diag = lane == sub
    later = (lax.broadcasted_iota(jnp.int32, (page, page), 0)
             > lax.broadcasted_iota(jnp.int32, (page, page), 1)).astype(F32)

    @pl.when(g == 0)
    def _():
        m_scr[...] = jnp.full_like(m_scr, -jnp.inf)
        l_scr[...] = jnp.zeros_like(l_scr)
        acc_scr[...] = jnp.zeros_like(acc_scr)
        cur = jnp.where(diag, jnp.broadcast_to(lfc_ref[0], (N_HEADS, 128)), 0.0)
        carry_scr[...] = jnp.broadcast_to(jnp.sum(cur, axis=-1, keepdims=True) * LOG2E, (N_HEADS, 128))

    m_run = m_scr[...]
    l_run = l_scr[...]
    acc_run = acc_scr[...]
    for pi in range(pp):
        x = lf_refs[pi][...] * LOG2E
        hi, mid, lo = _split3(x)
        carry = carry_scr[...]
        bias = (_dot(hi.astype(F32), later) + _dot(mid.astype(F32), later)
                + _dot(lo.astype(F32), later) + carry)
        carry_scr[...] = carry + jnp.sum(x, axis=-1, keepdims=True)
        wide_scr[pi] = jnp.concatenate([bias, jnp.zeros((page - N_HEADS, page), F32)], axis=0).T
        k_ref, v_ref = k_refs[pi], v_refs[pi]
        for t0 in range(0, page, group):
            ss = []
            for t in range(t0, t0 + group):
                b_t = jnp.where(diag, jnp.broadcast_to(wide_scr[pi, t:t + 1, :], (N_HEADS, 128)), 0.0)
                s_t = jnp.sum(k_ref[t] * q + b_t, axis=-1, keepdims=True)
                ss.append(jnp.broadcast_to(s_t, (N_HEADS, 128)))
            m_new = jnp.maximum(m_run, _tree(jnp.maximum, ss))
            alpha = jnp.exp2(m_run - m_new)
            ps = [jnp.exp2(s_t - m_new) for s_t in ss]
            l_run = alpha * l_run + _tree(jnp.add, ps)
            acc_run = alpha * acc_run + _tree(jnp.add, [p_t * v_ref[t0 + n] for n, p_t in enumerate(ps)])
            m_run = m_new

    m_scr[...] = m_run
    l_scr[...] = l_run
    acc_scr[...] = acc_run

    @pl.when(g == n_g - 1)
    def _():
        s_cur = jnp.sum(kc_ref[0] * q, axis=-1, keepdims=True)
        m_new = jnp.maximum(m_run, s_cur)
        alpha = jnp.exp2(m_run - m_new)
        p = jnp.exp2(s_cur - m_new)
        l = alpha * l_run + p
        acc = alpha * acc_run + p * vc_ref[0]
        o_ref[0] = acc * (1.0 / l)


def _decode(page_table, q_s, k_cur, v_cur, lf_cur, cache_k, cache_v, cache_lft, *, pp, group):
    b, n_pages = page_table.shape
    page = cache_lft.shape[-1]
    n_g = n_pages // pp
    kernel = functools.partial(_decode_kernel, pp=pp, group=group)
    seq_blk = lambda w: pl.BlockSpec((1, w, 128), lambda s, g, pt: (s, 0, 0))

    def page_map(pi, nd):
        def index_map(s, g, pt):
            return (pt[s * n_pages + (n_pages - 1 - (g * pp + pi))],) + (0,) * nd
        return index_map

    k_specs = [pl.BlockSpec((None, page, N_HEADS, D_HEAD), page_map(pi, 3)) for pi in range(pp)]
    v_specs = [pl.BlockSpec((None, page, N_HEADS, D_HEAD), page_map(pi, 3)) for pi in range(pp)]
    lf_specs = [pl.BlockSpec((None, N_HEADS, page), page_map(pi, 2)) for pi in range(pp)]
    grid_spec = pltpu.PrefetchScalarGridSpec(
        num_scalar_prefetch=1,
        grid=(b, n_g),
        in_specs=[seq_blk(N_HEADS), seq_blk(N_HEADS), seq_blk(N_HEADS), seq_blk(1)]
                 + k_specs + v_specs + lf_specs,
        out_specs=pl.BlockSpec((1, N_HEADS, D_HEAD), lambda s, g, pt: (s, 0, 0)),
        scratch_shapes=[pltpu.VMEM((N_HEADS, 128), F32), pltpu.VMEM((N_HEADS, 128), F32),
                        pltpu.VMEM((N_HEADS, D_HEAD), F32), pltpu.VMEM((N_HEADS, 128), F32),
                        pltpu.VMEM((pp, page, page), F32)],
    )
    return pl.pallas_call(
        kernel,
        grid_spec=grid_spec,
        out_shape=jax.ShapeDtypeStruct((b, N_HEADS, D_HEAD), F32),
        compiler_params=pltpu.CompilerParams(
            dimension_semantics=("arbitrary", "arbitrary"), vmem_limit_bytes=VMEM_LIMIT),
        name="fox_decode",
    )(page_table.reshape(-1), q_s, k_cur, v_cur, lf_cur,
      *([cache_k] * pp), *([cache_v] * pp), *([cache_lft] * pp))


def _gelu_tanh(y):
    return 0.5 * y * (1.0 + jnp.tanh(math.sqrt(2.0 / math.pi) * (y + 0.044715 * (y * y * y))))


def _ssm_kernel(u_ref, bre_ref, bim_ref, cre_ref, cim_ref,
                are_ref, aim_ref, akre_ref, akim_ref, d_ref, wg_ref, bg_ref, h0re_ref, h0im_ref,
                o_ref, hpre_ref, hpim_ref, hsre_ref, hsim_ref,
                sre_scr, sim_scr, endre_scr, endim_scr, hinre_scr, hinim_scr, cre_scr, cim_scr,
                *, n_prompt_chunks, steps, slabs_per_pass, n_sample):
    c = pl.program_id(0)
    n_slab = sre_scr.shape[0]
    n_blk = bre_ref.shape[0]
    blk_slabs = n_slab // n_blk
    lanes = lambda sb: slice(sb * 128, (sb + 1) * 128)

    def input_proj(ub):
        for j in range(n_blk):
            uj = ub[:, j * 128:(j + 1) * 128]
            zr = _dot(uj, bre_ref[j])
            zi = _dot(uj, bim_ref[j])
            for i in range(blk_slabs):
                sre_scr[j * blk_slabs + i] = zr[:, lanes(i)]
                sim_scr[j * blk_slabs + i] = zi[:, lanes(i)]

    def tail(uf):
        ys = []
        for j in range(n_blk):
            sbs = range(j * blk_slabs, (j + 1) * blk_slabs)
            hr = jnp.concatenate([sre_scr[sb] for sb in sbs], axis=-1).astype(BF16)
            hi = jnp.concatenate([sim_scr[sb] for sb in sbs], axis=-1).astype(BF16)
            ys.append(_dot(hr, cre_ref[j]) + _dot(hi, cim_ref[j]))
        y = jnp.concatenate(ys, axis=-1) + d_ref[...] * uf
        gl = _gelu_tanh(y)
        return gl * jax.nn.sigmoid(_dot(gl.astype(BF16), wg_ref[...]) + bg_ref[...])

    def scan_pass(store):
        for g0 in range(0, n_slab, slabs_per_pass):
            sbs = list(range(g0, g0 + slabs_per_pass))
            ar = [jnp.broadcast_to(are_ref[:, lanes(sb)], (8, 128)) for sb in sbs]
            ai = [jnp.broadcast_to(aim_ref[:, lanes(sb)], (8, 128)) for sb in sbs]
            if store:
                init = ([hinre_scr[:, lanes(sb)] for sb in sbs], [hinim_scr[:, lanes(sb)] for sb in sbs])
            else:
                zero = jnp.zeros((8, 128), F32)
                init = ([zero] * len(sbs), [zero] * len(sbs))

            def body(k, carry, sbs=sbs, ar=ar, ai=ai):
                re, im = carry
                rows = pl.ds(k, 8, stride=steps)
                nre, nim = [], []
                for n, sb in enumerate(sbs):
                    r = ar[n] * re[n] - ai[n] * im[n] + sre_scr[sb, rows, :]
                    i = ar[n] * im[n] + ai[n] * re[n] + sim_scr[sb, rows, :]
                    if store:
                        sre_scr[sb, rows, :] = r
                        sim_scr[sb, rows, :] = i
                    nre.append(r)
                    nim.append(i)
                return nre, nim

            re, im = lax.fori_loop(0, steps, body, init, unroll=3)
            if not store:
                for n, sb in enumerate(sbs):
                    endre_scr[:, lanes(sb)] = re[n]
                    endim_scr[:, lanes(sb)] = im[n]

    @pl.when(c == 0)
    def _():
        cre_scr[...] = jnp.zeros_like(cre_scr)
        cim_scr[...] = jnp.zeros_like(cim_scr)

    @pl.when(c < n_prompt_chunks)
    def _():
        uf = u_ref[...]
        input_proj(uf.astype(BF16))
        scan_pass(False)
        cr = cre_scr[...]
        ci = cim_scr[...]
        akr = akre_ref[...]
        aki = akim_ref[...]
        for s in range(8):
            hinre_scr[s:s + 1, :] = cr
            hinim_scr[s:s + 1, :] = ci
            er = endre_scr[s:s + 1, :]
            ei = endim_scr[s:s + 1, :]
            cr, ci = er + (akr * cr - aki * ci), ei + (akr * ci + aki * cr)
        cre_scr[...] = cr
        cim_scr[...] = ci
        hpre_ref[...] = cr
        hpim_ref[...] = ci
        scan_pass(True)
        o_ref[...] = tail(uf).astype(o_ref.dtype)

    @pl.when(c == n_prompt_chunks)
    def _():
        uf = u_ref[...]
        input_proj(uf.astype(BF16))
        rs = slice(SAMPLE_OFF, SAMPLE_OFF + n_sample)
        for sb in range(n_slab):
            ar = are_ref[:, lanes(sb)]
            ai = aim_ref[:, lanes(sb)]
            h0r = h0re_ref[:, lanes(sb)]
            h0i = h0im_ref[:, lanes(sb)]
            nre = ar * h0r - ai * h0i + sre_scr[sb, rs, :]
            nim = ar * h0i + ai * h0r + sim_scr[sb, rs, :]
            sre_scr[sb, rs, :] = nre
            sim_scr[sb, rs, :] = nim
            hsre_ref[:, lanes(sb)] = nre
            hsim_ref[:, lanes(sb)] = nim
        o_ref[...] = tail(uf).astype(o_ref.dtype)


def _ssm(u, bre, bim, cre, cim, are, aim, akre, akim, d_skip, w_glu, b_glu, h0re, h0im,
         *, t, n_prompt_chunks, slabs_per_pass):
    rows = u.shape[0]
    n_state = are.shape[1]
    n_sample = h0re.shape[0]
    full = lambda a: pl.BlockSpec(a.shape, lambda c: (0,) * a.ndim)
    kernel = functools.partial(_ssm_kernel, n_prompt_chunks=n_prompt_chunks, steps=t // 8,
                               slabs_per_pass=slabs_per_pass, n_sample=n_sample)
    state_row = jax.ShapeDtypeStruct((1, n_state), F32)
    state_smp = jax.ShapeDtypeStruct((n_sample, n_state), F32)
    consts = (bre, bim, cre, cim, are, aim, akre, akim, d_skip, w_glu, b_glu, h0re, h0im)
    return pl.pallas_call(
        kernel,
        grid=(n_prompt_chunks + 1,),
        in_specs=[pl.BlockSpec((t, u.shape[1]), lambda c: (c, 0))] + [full(a) for a in consts],
        out_specs=[pl.BlockSpec((t, u.shape[1]), lambda c: (c, 0)),
                   full(state_row), full(state_row), full(state_smp), full(state_smp)],
        out_shape=[jax.ShapeDtypeStruct((rows, u.shape[1]), BF16),
                   state_row, state_row, state_smp, state_smp],
        scratch_shapes=[pltpu.VMEM((n_state // 128, t, 128), F32), pltpu.VMEM((n_state // 128, t, 128), F32),
                        pltpu.VMEM((8, n_state), F32), pltpu.VMEM((8, n_state), F32),
                        pltpu.VMEM((8, n_state), F32), pltpu.VMEM((8, n_state), F32),
                        pltpu.VMEM((1, n_state), F32), pltpu.VMEM((1, n_state), F32)],
        compiler_params=pltpu.CompilerParams(
            dimension_semantics=("arbitrary",), vmem_limit_bytes=VMEM_LIMIT),
        name="s5_branch",
    )(u, *consts)


def _merge_kernel(attn_ref, as_ref, ssm_ref, sga_ref, sgs_ref, x_ref, wba_ref, wbs_ref, wo_ref, g2_ref,
                  x1_ref, h2_ref, ma_scr, *, sample_tile):
    n_sample = as_ref.shape[0]
    ma_scr[...] = _dot(attn_ref[...], wba_ref[...])

    @pl.when(pl.program_id(0) == sample_tile)
    def _():
        acc = jnp.zeros((n_sample, wba_ref.shape[1]), F32)
        for h in range(N_HEADS):
            acc = acc + _dot(as_ref[:, h, :].astype(BF16), wba_ref[h * D_HEAD:(h + 1) * D_HEAD, :])
        ma_scr[SAMPLE_OFF:SAMPLE_OFF + n_sample, :] = acc

    m = (sga_ref[...].astype(F32) * ma_scr[...]
         + sgs_ref[...].astype(F32) * _dot(ssm_ref[...], wbs_ref[...]))
    x1 = x_ref[...] + _dot(m.astype(BF16), wo_ref[...])
    x1_ref[...] = x1
    ms = jnp.mean(x1 * x1, axis=-1, keepdims=True)
    h2_ref[...] = ((x1 * lax.rsqrt(ms + EPS)) * g2_ref[...]).astype(BF16)


def _merge(attn, attn_s, ssm, sga, sgs, x_all, wba, wbs, wo, g2, *, tm, sample_tile):
    rows, d = x_all.shape
    row_blk = lambda w: pl.BlockSpec((tm, w), lambda i: (i, 0))
    const = lambda a: pl.BlockSpec(a.shape, lambda i: (0,) * a.ndim, pipeline_mode=pl.Buffered(1))
    kernel = functools.partial(_merge_kernel, sample_tile=sample_tile)
    return pl.pallas_call(
        kernel,
        grid=(rows // tm,),
        in_specs=[row_blk(D_ATTN), const(attn_s), row_blk(ssm.shape[1]), row_blk(d), row_blk(d), row_blk(d),
                  const(wba), const(wbs), const(wo), const(g2)],
        out_specs=[row_blk(d), row_blk(d)],
        out_shape=[jax.ShapeDtypeStruct((rows, d), F32), jax.ShapeDtypeStruct((rows, d), BF16)],
        scratch_shapes=[pltpu.VMEM((tm, d), F32)],
        compiler_params=pltpu.CompilerParams(
            dimension_semantics=("arbitrary",), vmem_limit_bytes=VMEM_LIMIT),
        name="merge",
    )(attn, attn_s, ssm, sga, sgs, x_all, wba, wbs, wo, g2)


def _ffn_kernel(h2_ref, x1_ref, w1_ref, w3_ref, w2_ref, o_ref):
    @pl.when(pl.program_id(1) == 0)
    def _():
        o_ref[...] = x1_ref[...]

    h2 = h2_ref[...]
    a = _dot(h2, w1_ref[...])
    b = _dot(h2, w3_ref[...])
    o_ref[...] += _dot(((a * jax.nn.sigmoid(a)) * b).astype(BF16), w2_ref[...])


def _ffn(h2, x1, w1, w3, w2, *, tm, tf):
    rows, d = x1.shape
    d_ff = w1.shape[1]
    return pl.pallas_call(
        _ffn_kernel,
        grid=(rows // tm, d_ff // tf),
        in_specs=[pl.BlockSpec((tm, d), lambda i, f: (i, 0)),
                  pl.BlockSpec((tm, d), lambda i, f: (i, 0)),
                  pl.BlockSpec((d, tf), lambda i, f: (0, f)),
                  pl.BlockSpec((d, tf), lambda i, f: (0, f)),
                  pl.BlockSpec((tf, d), lambda i, f: (f, 0))],
        out_specs=pl.BlockSpec((tm, d), lambda i, f: (i, 0)),
        out_shape=jax.ShapeDtypeStruct((rows, d), F32),
        compiler_params=pltpu.CompilerParams(
            dimension_semantics=("arbitrary", "arbitrary"), vmem_limit_bytes=VMEM_LIMIT),
        name="ffn",
    )(h2, x1, w1, w3, w2)


def _ssm_tables(lam_re, lam_im, log_dt, b_re, b_im, c_re, c_im, steps):
    n_groups = lam_re.shape[0]
    lr, li = lam_re.astype(F32), lam_im.astype(F32)
    dt = jnp.exp(log_dt.astype(F32))[:, None]
    mag = jnp.exp(lr * dt)
    ar, ai = mag * jnp.cos(li * dt), mag * jnp.sin(li * dt)
    den = lr * lr + li * li
    cr = ((ar - 1.0) * lr + ai * li) / den
    ci = (ai * lr - (ar - 1.0) * li) / den
    br, bi = b_re.astype(F32), b_im.astype(F32)
    bbr = cr[..., None] * br - ci[..., None] * bi
    bbi = cr[..., None] * bi + ci[..., None] * br
    akr, aki = ar, ai
    for _ in range(steps - 1):
        akr, aki = akr * ar - aki * ai, akr * ai + aki * ar
    n_blk = n_groups // GROUPS_PER_BLOCK
    eye = jnp.eye(GROUPS_PER_BLOCK, dtype=F32)

    def in_blocks(m):
        m = m.reshape(n_blk, GROUPS_PER_BLOCK, P_STATE, SSM_GROUP)
        return jnp.einsum('jgpc,gh->jgchp', m, eye).reshape(
            n_blk, GROUPS_PER_BLOCK * SSM_GROUP, GROUPS_PER_BLOCK * P_STATE).astype(BF16)

    def out_blocks(m):
        m = m.reshape(n_blk, GROUPS_PER_BLOCK, SSM_GROUP, P_STATE)
        return jnp.einsum('jgcp,gh->jgphc', m, eye).reshape(
            n_blk, GROUPS_PER_BLOCK * P_STATE, GROUPS_PER_BLOCK * SSM_GROUP).astype(BF16)

    flat = lambda a: a.reshape(1, -1)
    return (in_blocks(bbr), in_blocks(bbi),
            out_blocks(c_re.astype(F32)), out_blocks(-c_im.astype(F32)),
            flat(ar), flat(ai), flat(akr), flat(aki))


def _tiles(l_prompt):
    if l_prompt == 8208:
        return dict(t_ssm=432, t_attn=640, rc=32, tm_ffn=864, tf=512, slabs_per_pass=4, pp=16, group=32)
    return dict(t_ssm=200, t_attn=128, rc=32, tm_ffn=300, tf=512, slabs_per_pass=4, pp=2, group=32)


def kernel(x_prompt, x_sample, cache_k, cache_v, cache_logf, state_ssm, page_table, meta, norm1_g, w_in, b_f, q_norm_g, k_norm_g, lam_re, lam_im, log_dt, b_re, b_im, c_re, c_im, d_skip, w_glu, b_glu, w_br_attn, w_br_ssm, w_out, norm2_g, w_ff1, w_ff3, w_ff2):
    assert x_prompt.shape[0] == 1 and x_sample.shape[1] == 1 and w_in.shape[0] == 1
    n_meta, d = meta.shape
    seq = x_prompt.shape[1]
    n_sample = x_sample.shape[0]
    l_prompt = n_meta + seq
    cfg = _tiles(l_prompt)
    t_ssm = cfg["t_ssm"]
    assert l_prompt % t_ssm == 0 and SAMPLE_OFF + n_sample <= t_ssm
    n_p = l_prompt // t_ssm
    rows = l_prompt + t_ssm
    s0 = l_prompt + SAMPLE_OFF
    d_ssm = w_glu.shape[1]
    n_groups = d_ssm // SSM_GROUP

    x_all = jnp.concatenate([
        meta.astype(F32), x_prompt[0], jnp.zeros((SAMPLE_OFF, d), F32), x_sample[:, 0],
        jnp.zeros((rows - s0 - n_sample, d), F32)], axis=0)
    w = w_in[0]
    c0, c1 = 3 * D_ATTN, 3 * D_ATTN + N_HEADS
    n_w1 = w.shape[1] - N_HEADS
    col = lax.broadcasted_iota(jnp.int32, (1, n_w1), 1)
    w1 = jnp.where(col < c0, w[:, :n_w1], w[:, N_HEADS:]).astype(BF16)
    wf = jnp.pad(w[:, c0:c1], ((0, 0), (0, 128 - N_HEADS))).astype(BF16)
    bf = jnp.pad(b_f[0].astype(F32), (0, 128 - N_HEADS)).reshape(1, 128)
    row = lambda a: a.astype(F32).reshape(1, -1)

    q, kf, ks, kb, vf, vs, vb, u, sga, sgs, lf = _inproj(
        x_all, row(norm1_g[0]), w1, wf, bf, row(q_norm_g[0]), row(k_norm_g[0]),
        tm=t_ssm, l_prompt=l_prompt, n_sample=n_sample)

    ta = cfg["t_attn"]
    nk = -(-l_prompt // ta)
    assert nk * ta <= rows
    c_rows = _cumsum_rows(lf, n_rows=nk * ta, t=ta)
    attn = _flash(q, kb, vb, c_rows.reshape(N_HEADS, nk, 1, ta), t=ta, rc=cfg["rc"])

    smp = slice(s0, s0 + n_sample)
    q_s = q[smp].astype(F32).reshape(n_sample, N_HEADS, D_HEAD)
    lf_cur = lf[smp].reshape(n_sample, 1, 128)
    attn_s = _decode(page_table, q_s, ks, vs, lf_cur, cache_k[0], cache_v[0],
                     jnp.swapaxes(cache_logf[0], 1, 2), pp=cfg["pp"], group=cfg["group"])

    steps = t_ssm // 8
    bre, bim, cre, cim, are, aim, akre, akim = _ssm_tables(
        lam_re[0], lam_im[0], log_dt[0], b_re[0], b_im[0], c_re[0], c_im[0], steps)
    h0 = state_ssm[0].astype(F32).reshape(n_sample, n_groups * P_STATE, 2)
    ssm, hpre, hpim, hsre, hsim = _ssm(
        u, bre, bim, cre, cim, are, aim, akre, akim,
        row(d_skip[0]), w_glu[0].astype(BF16), row(b_glu[0]), h0[..., 0], h0[..., 1],
        t=t_ssm, n_prompt_chunks=n_p, slabs_per_pass=cfg["slabs_per_pass"])

    x1, h2 = _merge(attn, attn_s, ssm, sga, sgs, x_all, w_br_attn[0].astype(BF16),
                    w_br_ssm[0].astype(BF16), w_out[0].astype(BF16), row(norm2_g[0]),
                    tm=t_ssm, sample_tile=n_p)
    y = _ffn(h2, x1, w_ff1[0].astype(BF16), w_ff3[0].astype(BF16), w_ff2[0].astype(BF16),
             tm=cfg["tm_ffn"], tf=cfg["tf"])

    state = lambda re, im, n: jnp.stack([re, im], axis=-1).reshape(1, n, n_groups, P_STATE, 2)
    return (y[n_meta:l_prompt][None],
            y[smp][:, None],
            kf.reshape(1, 1, l_prompt, N_HEADS, D_HEAD),
            vf.reshape(1, 1, l_prompt, N_HEADS, D_HEAD),
            lf[:l_prompt, :N_HEADS].reshape(1, 1, l_prompt, N_HEADS),
            state(hpre, hpim, 1),
            ks.reshape(1, n_sample, 1, N_HEADS, D_HEAD),
            vs.reshape(1, n_sample, 1, N_HEADS, D_HEAD),
            lf[smp, :N_HEADS].reshape(1, n_sample, 1, N_HEADS),
            state(hsre, hsim, n_sample))
```

```python
import functools
import math

import jax
import jax.numpy as jnp
from jax import lax
from jax.experimental import pallas as pl
from jax.experimental.pallas import tpu as pltpu

F32 = jnp.float32
BF16 = jnp.bfloat16

N_HEADS = 8
D_HEAD = 128
D_ATTN = N_HEADS * D_HEAD
SSM_GROUP = 16
P_STATE = 64
EPS = 1e-6
SAMPLE_OFF = 16
GROUPS_PER_BLOCK = 8
VMEM_LIMIT = 56 * 1024 * 1024
LOG2E = math.log2(math.e)


def _dot(a, b):
    return jnp.dot(a, b, preferred_element_type=F32)


def _split3(x):
    hi = x.astype(BF16)
    r1 = x - hi.astype(F32)
    mid = r1.astype(BF16)
    lo = (r1 - mid.astype(F32)).astype(BF16)
    return hi, mid, lo


def _tree(op, xs):
    xs = list(xs)
    while len(xs) > 1:
        xs = [op(xs[i], xs[i + 1]) for i in range(0, len(xs) - 1, 2)] + ([xs[-1]] if len(xs) % 2 else [])
    return xs[0]


def _log_sigmoid(f):
    return jnp.minimum(f, 0.0) - jnp.log1p(jnp.exp(-jnp.abs(f)))


def _head_rmsnorm(z, g):
    outs = []
    for h in range(N_HEADS):
        zh = z[:, h * D_HEAD:(h + 1) * D_HEAD]
        ms = jnp.broadcast_to(jnp.mean(zh * zh, axis=-1, keepdims=True), zh.shape)
        outs.append(zh * lax.rsqrt(ms + EPS) * g)
    return jnp.concatenate(outs, axis=-1)


def _assemble_rows(i, n_prompt_tiles, xp_ref, xh_ref, meta_ref, xs_ref, x_scr):
    n_meta = meta_ref.shape[0]
    n_sample = xs_ref.shape[0]
    body = x_scr.shape[0] - n_meta

    @pl.when(i == 0)
    def _():
        x_scr[:n_meta, :] = meta_ref[...]
        x_scr[n_meta:, :] = xp_ref[:body, :]

    @pl.when(jnp.logical_and(i > 0, i < n_prompt_tiles))
    def _():
        x_scr[:n_meta, :] = xh_ref[...]
        x_scr[n_meta:, :] = xp_ref[:body, :]

    @pl.when(i == n_prompt_tiles)
    def _():
        x_scr[...] = jnp.zeros_like(x_scr)
        x_scr[SAMPLE_OFF:SAMPLE_OFF + n_sample, :] = xs_ref[...]


def _inproj_kernel(xp_ref, xh_ref, meta_ref, xs_ref, g1_ref, w_ref, wf_ref, bf_ref, qg_ref, kg_ref,
                   q_ref, kf_ref, ks_ref, kb_ref, vf_ref, vs_ref, vb_ref, u_ref, sga_ref, sgs_ref, lf_ref,
                   h_scr, x_scr, *, n_prompt_tiles, n_sample):
    i = pl.program_id(0)
    j = pl.program_id(1)
    smp = slice(SAMPLE_OFF, SAMPLE_OFF + n_sample)

    @pl.when(j == 0)
    def _():
        _assemble_rows(i, n_prompt_tiles, xp_ref, xh_ref, meta_ref, xs_ref, x_scr)
        x = x_scr[...]
        ms = jnp.mean(x * x, axis=-1, keepdims=True)
        hb = ((x * lax.rsqrt(ms + EPS)) * g1_ref[...]).astype(BF16)
        h_scr[...] = hb
        lf_ref[...] = _log_sigmoid(_dot(hb, wf_ref[...]) + bf_ref[...])

    def proj():
        return _dot(h_scr[...], w_ref[...])

    def store_heads(val, full_ref, smp_ref):
        @pl.when(i < n_prompt_tiles)
        def _():
            full_ref[...] = val

        @pl.when(i == n_prompt_tiles)
        def _():
            for h in range(N_HEADS):
                smp_ref[:, h, :] = val[smp, h * D_HEAD:(h + 1) * D_HEAD]

    @pl.when(j == 0)
    def _():
        q_ref[...] = (_head_rmsnorm(proj(), qg_ref[...]) * (D_HEAD ** -0.5 * LOG2E)).astype(BF16)

    @pl.when(j == 1)
    def _():
        kn = _head_rmsnorm(proj(), kg_ref[...])
        kb_ref[...] = kn.astype(BF16)
        store_heads(kn, kf_ref, ks_ref)

    @pl.when(j == 2)
    def _():
        z = proj()
        vb_ref[...] = z.astype(BF16)
        store_heads(z, vf_ref, vs_ref)

    @pl.when(j == 3)
    def _():
        u_ref[...] = proj()

    for jj, ref, half in ((4, sga_ref, 0), (5, sga_ref, 1), (6, sgs_ref, 0), (7, sgs_ref, 1)):
        @pl.when(j == jj)
        def _(ref=ref, half=half):
            ref[:, half * 1024:(half + 1) * 1024] = jax.nn.sigmoid(proj()).astype(BF16)


def _prompt_row_specs(tm, d, n_meta, n_prompt_tiles, grid_rank):
    del grid_rank
    assert tm % n_meta == 0
    per = tm // n_meta
    main = lambda i, *_: (jnp.minimum(i, n_prompt_tiles - 1), 0)
    head = lambda i, *_: (jnp.maximum(jnp.minimum(i, n_prompt_tiles - 1) * per - 1, 0), 0)
    return [pl.BlockSpec((tm, d), main), pl.BlockSpec((n_meta, d), head)]


def _inproj(x_prompt, meta, x_sample, g1, w1, wf, bf, qg, kg, *, tm, rows):
    n_meta, d = meta.shape
    n_sample = x_sample.shape[0]
    l_prompt = n_meta + x_prompt.shape[0]
    n_col = w1.shape[1] // 1024
    n_p = l_prompt // tm
    row_blk = lambda w: pl.BlockSpec((tm, w), lambda i, j: (i, 0))
    full = lambda a: pl.BlockSpec(a.shape, lambda i, j: (0,) * a.ndim)
    heads_p = pl.BlockSpec((tm, D_ATTN), lambda i, j: (jnp.minimum(i, n_p - 1), 0))
    heads_s = pl.BlockSpec((n_sample, N_HEADS, D_HEAD), lambda i, j: (0, 0, 0))
    f32_p = jax.ShapeDtypeStruct((l_prompt, D_ATTN), F32)
    f32_s = jax.ShapeDtypeStruct((n_sample, N_HEADS, D_HEAD), F32)
    outs = [
        (jax.ShapeDtypeStruct((rows, D_ATTN), BF16), row_blk(D_ATTN)),
        (f32_p, heads_p), (f32_s, heads_s),
        (jax.ShapeDtypeStruct((rows, D_ATTN), BF16), row_blk(D_ATTN)),
        (f32_p, heads_p), (f32_s, heads_s),
        (jax.ShapeDtypeStruct((rows, D_ATTN), BF16), row_blk(D_ATTN)),
        (jax.ShapeDtypeStruct((rows, 1024), F32), row_blk(1024)),
        (jax.ShapeDtypeStruct((rows, d), BF16), row_blk(d)),
        (jax.ShapeDtypeStruct((rows, d), BF16), row_blk(d)),
        (jax.ShapeDtypeStruct((rows, 128), F32), row_blk(128)),
    ]
    kernel = functools.partial(_inproj_kernel, n_prompt_tiles=n_p, n_sample=n_sample)
    return pl.pallas_call(
        kernel,
        grid=(rows // tm, n_col),
        in_specs=_prompt_row_specs(tm, d, n_meta, n_p, 2) + [
                  full(meta), full(x_sample), full(g1), pl.BlockSpec((d, 1024), lambda i, j: (0, j)),
                  full(wf), full(bf), full(qg), full(kg)],
        out_specs=[o[1] for o in outs],
        out_shape=[o[0] for o in outs],
        scratch_shapes=[pltpu.VMEM((tm, d), BF16), pltpu.VMEM((tm, d), F32)],
        compiler_params=pltpu.CompilerParams(
            dimension_semantics=("arbitrary", "arbitrary"), vmem_limit_bytes=VMEM_LIMIT),
        name="inproj",
    )(x_prompt, x_prompt, meta, x_sample, g1, w1, wf, bf, qg, kg)


def _cumsum_kernel(lf_ref, crow_ref, carry_scr):
    @pl.when(pl.program_id(0) == 0)
    def _():
        carry_scr[...] = jnp.zeros_like(carry_scr)

    lf = lf_ref[...]
    t = lf.shape[0]
    r = lax.broadcasted_iota(jnp.int32, (t, t), 0)
    c = lax.broadcasted_iota(jnp.int32, (t, t), 1)
    tri = (c <= r).astype(BF16)
    hi, mid, lo = _split3(lf)
    cs = _dot(tri, hi) + _dot(tri, mid) + _dot(tri, lo) + carry_scr[...]
    carry_scr[...] = cs[t - 1:t, :]
    crow_ref[...] = (cs * LOG2E).T[:N_HEADS, :]


def _cumsum_rows(lf, *, n_rows, t):
    return pl.pallas_call(
        _cumsum_kernel,
        grid=(n_rows // t,),
        in_specs=[pl.BlockSpec((t, 128), lambda i: (i, 0))],
        out_specs=pl.BlockSpec((N_HEADS, t), lambda i: (0, i)),
        out_shape=jax.ShapeDtypeStruct((N_HEADS, n_rows), F32),
        scratch_shapes=[pltpu.VMEM((1, 128), F32)],
        compiler_params=pltpu.CompilerParams(dimension_semantics=("arbitrary",)),
        name="logf_cumsum",
    )(lf)


def _flash_kernel(q_ref, k_ref, v_ref, c_ref, zero_ref, o_ref,
                  s_scr, p_scr, m_scr, l_scr, a_scr, acc_scr, *, t, rc):
    del zero_ref
    i = pl.program_id(1)
    q = q_ref[...]
    m_scr[...] = jnp.full_like(m_scr, -jnp.inf)
    l_scr[...] = jnp.zeros_like(l_scr)
    acc_scr[...] = jnp.zeros_like(acc_scr)

    def scores(j, slot):
        k0 = pl.multiple_of(j * t, t)
        s_scr[slot] = lax.dot_general(q, k_ref[pl.ds(k0, t), :], (((1,), (1,)), ((), ())),
                                      preferred_element_type=F32)

    def weighted_values(j, slot):
        k0 = pl.multiple_of(j * t, t)
        acc_scr[...] = a_scr[slot] * acc_scr[...] + _dot(p_scr[slot], v_ref[pl.ds(k0, t), :])

    def softmax(j, slot, masked):
        cj = c_ref[j]

        def chunk(r, carry):
            rows = pl.ds(pl.multiple_of(r * rc, rc), rc)
            s = s_scr[slot, rows, :] - cj
            if masked:
                row = r * rc + lax.broadcasted_iota(jnp.int32, (rc, t), 0)
                col = lax.broadcasted_iota(jnp.int32, (rc, t), 1)
                s = jnp.where(col <= row, s, -jnp.inf)
            m_prev = m_scr[rows, :]
            m_new = jnp.maximum(m_prev, jnp.broadcast_to(jnp.max(s, axis=-1, keepdims=True), (rc, D_HEAD)))
            alpha = jnp.exp2(m_prev - m_new)
            p = jnp.exp2(s - jnp.tile(m_new, (1, t // D_HEAD)))
            l_scr[rows, :] = alpha * l_scr[rows, :] + jnp.broadcast_to(
                jnp.sum(p, axis=-1, keepdims=True), (rc, D_HEAD))
            m_scr[rows, :] = m_new
            a_scr[slot, rows, :] = alpha
            p_scr[slot, rows, :] = p.astype(BF16)
            return carry

        lax.fori_loop(0, t // rc, chunk, 0, unroll=True)

    p_scr[1] = jnp.zeros((t, t), BF16)
    a_scr[1] = jnp.ones((t, D_HEAD), F32)
    scores(0, 0)

    def pair(jj, carry):
        j = 2 * jj

        @pl.when(jj >= 0)
        def _():
            scores(j + 1, 1)
            softmax(j, 0, False)
            weighted_values(jnp.maximum(j - 1, 0), 1)

        @pl.when(jj >= 0)
        def _():
            scores(j + 2, 0)
            softmax(j + 1, 1, False)
            weighted_values(j, 0)

        return carry

    n_pairs = i // 2
    lax.fori_loop(0, n_pairs, pair, 0)
    prev = jnp.maximum(2 * n_pairs - 1, 0)

    @pl.when(i % 2 == 0)
    def _():
        softmax(i, 0, True)
        weighted_values(prev, 1)
        weighted_values(i, 0)

    @pl.when(i % 2 == 1)
    def _():
        scores(i, 1)
        softmax(i - 1, 0, False)
        weighted_values(prev, 1)

    @pl.when(i % 2 == 1)
    def _():
        softmax(i, 1, True)
        weighted_values(i - 1, 0)
        weighted_values(i, 1)

    o_ref[...] = (acc_scr[...] * (1.0 / l_scr[...])).astype(o_ref.dtype)


def _flash(q, kb, vb, c_tiles, *, t, rc):
    rows = q.shape[0]
    nk = c_tiles.shape[1]
    kernel = functools.partial(_flash_kernel, t=t, rc=rc)
    return pl.pallas_call(
        kernel,
        grid=(N_HEADS, nk),
        in_specs=[pl.BlockSpec((t, D_HEAD), lambda h, i: (i, h)),
                  pl.BlockSpec((nk * t, D_HEAD), lambda h, i: (0, h)),
                  pl.BlockSpec((nk * t, D_HEAD), lambda h, i: (0, h)),
                  pl.BlockSpec((None, nk, 1, t), lambda h, i: (h, 0, 0, 0)),
                  pl.BlockSpec(memory_space=pl.ANY)],
        out_specs=pl.BlockSpec((t, D_HEAD), lambda h, i: (i, h)),
        out_shape=jax.ShapeDtypeStruct((rows, D_ATTN), BF16),
        input_output_aliases={4: 0},
        scratch_shapes=[pltpu.VMEM((2, t, t), F32), pltpu.VMEM((2, t, t), BF16),
                        pltpu.VMEM((t, D_HEAD), F32), pltpu.VMEM((t, D_HEAD), F32),
                        pltpu.VMEM((2, t, D_HEAD), F32), pltpu.VMEM((t, D_HEAD), F32)],
        compiler_params=pltpu.CompilerParams(
            dimension_semantics=("arbitrary", "arbitrary"), vmem_limit_bytes=VMEM_LIMIT),
        name="fox_prompt",
    )(q, kb, vb, c_tiles, jnp.zeros((rows, D_ATTN), BF16))


def _decode_kernel(pt_ref, q_ref, kc_ref, vc_ref, lfc_ref, *refs, pp, group):
    del pt_ref
    k_refs = refs[:pp]
    v_refs = refs[pp:2 * pp]
    lf_refs = refs[2 * pp:3 * pp]
    o_ref = refs[3 * pp]
    m_scr, l_scr, acc_scr, carry_scr, wide_scr = refs[3 * pp + 1:]
    g = pl.program_id(1)
    n_g = pl.num_programs(1)
    page = lf_refs[0].shape[-1]

    q = q_ref[0]
    lane = lax.broadcasted_iota(jnp.int32, (N_HEADS, 128), 1)
    sub = lax.broadcasted_iota(jnp.int32, (N_HEADS, 128), 0)
    diag = lane == sub
    later = (lax.broadcasted_iota(jnp.int32, (page, page), 0)
             > lax.broadcasted_iota(jnp.int32, (page, page), 1)).astype(F32)

    @pl.when(g == 0)
    def _():
        m_scr[...] = jnp.full_like(m_scr, -jnp.inf)
        l_scr[...] = jnp.zeros_like(l_scr)
        acc_scr[...] = jnp.zeros_like(acc_scr)
        cur = jnp.where(diag, jnp.broadcast_to(lfc_ref[0], (N_HEADS, 128)), 0.0)
        carry_scr[...] = jnp.broadcast_to(jnp.sum(cur, axis=-1, keepdims=True) * LOG2E, (N_HEADS, 128))

    m_run = m_scr[...]
    l_run = l_scr[...]
    acc_run = acc_scr[...]
    for pi in range(pp):
        x = lf_refs[pi][...] * LOG2E
        hi, mid, lo = _split3(x)
        carry = carry_scr[...]
        bias = (_dot(hi.astype(F32), later) + _dot(mid.astype(F32), later)
                + _dot(lo.astype(F32), later) + carry)
        carry_scr[...] = carry + jnp.sum(x, axis=-1, keepdims=True)
        wide_scr[pi] = jnp.concatenate([bias, jnp.zeros((page - N_HEADS, page), F32)], axis=0).T
        k_ref, v_ref = k_refs[pi], v_refs[pi]
        for t0 in range(0, page, group):
            ss = []
            for t in range(t0, t0 + group):
                b_t = jnp.where(diag, jnp.broadcast_to(wide_scr[pi, t:t + 1, :], (N_HEADS, 128)), 0.0)
                s_t = jnp.sum(k_ref[t] * q + b_t, axis=-1, keepdims=True)
                ss.append(jnp.broadcast_to(s_t, (N_HEADS, 128)))
            m_new = jnp.maximum(m_run, _tree(jnp.maximum, ss))
            alpha = jnp.exp2(m_run - m_new)
            ps = [jnp.exp2(s_t - m_new) for s_t in ss]
            l_run = alpha * l_run + _tree(jnp.add, ps)
            acc_run = alpha * acc_run + _tree(jnp.add, [p_t * v_ref[t0 + n] for n, p_t in enumerate(ps)])
            m_run = m_new

    m_scr[...] = m_run
    l_scr[...] = l_run
    acc_scr[...] = acc_run

    @pl.when(g == n_g - 1)
    def _():
        s_cur = jnp.sum(kc_ref[0] * q, axis=-1, keepdims=True)
        m_new = jnp.maximum(m_run, s_cur)
        alpha = jnp.exp2(m_run - m_new)
        p = jnp.exp2(s_cur - m_new)
        l = alpha * l_run + p
        acc = alpha * acc_run + p * vc_ref[0]
        o_ref[0] = acc * (1.0 / l)


def _decode(page_table, q_s, k_cur, v_cur, lf_cur, cache_k, cache_v, cache_lft, *, pp, group):
    b, n_pages = page_table.shape
    page = cache_lft.shape[-1]
    n_g = n_pages // pp
    kernel = functools.partial(_decode_kernel, pp=pp, group=group)
    seq_blk = lambda w: pl.BlockSpec((1, w, 128), lambda s, g, pt: (s, 0, 0))

    def page_map(pi, nd):
        def index_map(s, g, pt):
            return (pt[s * n_pages + (n_pages - 1 - (g * pp + pi))],) + (0,) * nd
        return index_map

    k_specs = [pl.BlockSpec((None, page, N_HEADS, D_HEAD), page_map(pi, 3)) for pi in range(pp)]
    v_specs = [pl.BlockSpec((None, page, N_HEADS, D_HEAD), page_map(pi, 3)) for pi in range(pp)]
    lf_specs = [pl.BlockSpec((None, N_HEADS, page), page_map(pi, 2)) for pi in range(pp)]
    grid_spec = pltpu.PrefetchScalarGridSpec(
        num_scalar_prefetch=1,
        grid=(b, n_g),
        in_specs=[seq_blk(N_HEADS), seq_blk(N_HEADS), seq_blk(N_HEADS), seq_blk(1)]
                 + k_specs + v_specs + lf_specs,
        out_specs=pl.BlockSpec((1, N_HEADS, D_HEAD), lambda s, g, pt: (s, 0, 0)),
        scratch_shapes=[pltpu.VMEM((N_HEADS, 128), F32), pltpu.VMEM((N_HEADS, 128), F32),
                        pltpu.VMEM((N_HEADS, D_HEAD), F32), pltpu.VMEM((N_HEADS, 128), F32),
                        pltpu.VMEM((pp, page, page), F32)],
    )
    return pl.pallas_call(
        kernel,
        grid_spec=grid_spec,
        out_shape=jax.ShapeDtypeStruct((b, N_HEADS, D_HEAD), F32),
        compiler_params=pltpu.CompilerParams(
            dimension_semantics=("arbitrary", "arbitrary"), vmem_limit_bytes=VMEM_LIMIT),
        name="fox_decode",
    )(page_table.reshape(-1), q_s, k_cur, v_cur, lf_cur,
      *([cache_k] * pp), *([cache_v] * pp), *([cache_lft] * pp))


def _gelu_tanh(y):
    return 0.5 * y * (1.0 + jnp.tanh(math.sqrt(2.0 / math.pi) * (y + 0.044715 * (y * y * y))))


def _ssm_kernel(u_ref, bre_ref, bim_ref, cre_ref, cim_ref,
                are_ref, aim_ref, akre_ref, akim_ref, d_ref, wg_ref, bg_ref, h0re_ref, h0im_ref,
                o_ref, hpre_ref, hpim_ref, hsre_ref, hsim_ref,
                sre_scr, sim_scr, endre_scr, endim_scr, hinre_scr, hinim_scr, cre_scr, cim_scr,
                *, n_prompt_chunks, steps, slabs_per_pass, n_sample):
    c = pl.program_id(0)
    n_slab = sre_scr.shape[0]
    n_blk = bre_ref.shape[0]
    blk_slabs = n_slab // n_blk
    lanes = lambda sb: slice(sb * 128, (sb + 1) * 128)

    def input_proj(ub):
        for j in range(n_blk):
            uj = ub[:, j * 128:(j + 1) * 128]
            zr = _dot(uj, bre_ref[j])
            zi = _dot(uj, bim_ref[j])
            for i in range(blk_slabs):
                sre_scr[j * blk_slabs + i] = zr[:, lanes(i)]
                sim_scr[j * blk_slabs + i] = zi[:, lanes(i)]

    def tail(uf):
        ys = []
        for j in range(n_blk):
            sbs = range(j * blk_slabs, (j + 1) * blk_slabs)
            hr = jnp.concatenate([sre_scr[sb] for sb in sbs], axis=-1).astype(BF16)
            hi = jnp.concatenate([sim_scr[sb] for sb in sbs], axis=-1).astype(BF16)
            ys.append(_dot(hr, cre_ref[j]) + _dot(hi, cim_ref[j]))
        y = jnp.concatenate(ys, axis=-1) + d_ref[...] * uf
        gl = _gelu_tanh(y)
        return gl * jax.nn.sigmoid(_dot(gl.astype(BF16), wg_ref[...]) + bg_ref[...])

    def scan_pass(store):
        for g0 in range(0, n_slab, slabs_per_pass):
            sbs = list(range(g0, g0 + slabs_per_pass))
            ar = [jnp.broadcast_to(are_ref[:, lanes(sb)], (8, 128)) for sb in sbs]
            ai = [jnp.broadcast_to(aim_ref[:, lanes(sb)], (8, 128)) for sb in sbs]
            if store:
                init = ([hinre_scr[:, lanes(sb)] for sb in sbs], [hinim_scr[:, lanes(sb)] for sb in sbs])
            else:
                zero = jnp.zeros((8, 128), F32)
                init = ([zero] * len(sbs), [zero] * len(sbs))

            def body(k, carry, sbs=sbs, ar=ar, ai=ai):
                re, im = carry
                rows = pl.ds(k, 8, stride=steps)
                nre, nim = [], []
                for n, sb in enumerate(sbs):
                    r = ar[n] * re[n] - ai[n] * im[n] + sre_scr[sb, rows, :]
                    i = ar[n] * im[n] + ai[n] * re[n] + sim_scr[sb, rows, :]
                    if store:
                        sre_scr[sb, rows, :] = r
                        sim_scr[sb, rows, :] = i
                    nre.append(r)
                    nim.append(i)
                return nre, nim

            re, im = lax.fori_loop(0, steps, body, init, unroll=3)
            if not store:
                for n, sb in enumerate(sbs):
                    endre_scr[:, lanes(sb)] = re[n]
                    endim_scr[:, lanes(sb)] = im[n]

    @pl.when(c == 0)
    def _():
        cre_scr[...] = jnp.zeros_like(cre_scr)
        cim_scr[...] = jnp.zeros_like(cim_scr)

    @pl.when(c < n_prompt_chunks)
    def _():
        uf = u_ref[...]
        input_proj(uf.astype(BF16))
        scan_pass(False)
        cr = cre_scr[...]
        ci = cim_scr[...]
        akr = akre_ref[...]
        aki = akim_ref[...]
        for s in range(8):
            hinre_scr[s:s + 1, :] = cr
            hinim_scr[s:s + 1, :] = ci
            er = endre_scr[s:s + 1, :]
            ei = endim_scr[s:s + 1, :]
            cr, ci = er + (akr * cr - aki * ci), ei + (akr * ci + aki * cr)
        cre_scr[...] = cr
        cim_scr[...] = ci
        hpre_ref[...] = cr
        hpim_ref[...] = ci
        scan_pass(True)
        o_ref[...] = tail(uf).astype(o_ref.dtype)

    @pl.when(c == n_prompt_chunks)
    def _():
        uf = u_ref[...]
        input_proj(uf.astype(BF16))
        rs = slice(SAMPLE_OFF, SAMPLE_OFF + n_sample)
        for sb in range(n_slab):
            ar = are_ref[:, lanes(sb)]
            ai = aim_ref[:, lanes(sb)]
            h0r = h0re_ref[:, lanes(sb)]
            h0i = h0im_ref[:, lanes(sb)]
            nre = ar * h0r - ai * h0i + sre_scr[sb, rs, :]
            nim = ar * h0i + ai * h0r + sim_scr[sb, rs, :]
            sre_scr[sb, rs, :] = nre
            sim_scr[sb, rs, :] = nim
            hsre_ref[:, lanes(sb)] = nre
            hsim_ref[:, lanes(sb)] = nim
        o_ref[...] = tail(uf).astype(o_ref.dtype)


def _ssm(u, bre, bim, cre, cim, are, aim, akre, akim, d_skip, w_glu, b_glu, h0re, h0im,
         *, t, n_prompt_chunks, slabs_per_pass):
    rows = u.shape[0]
    n_state = are.shape[1]
    n_sample = h0re.shape[0]
    full = lambda a: pl.BlockSpec(a.shape, lambda c: (0,) * a.ndim)
    kernel = functools.partial(_ssm_kernel, n_prompt_chunks=n_prompt_chunks, steps=t // 8,
                               slabs_per_pass=slabs_per_pass, n_sample=n_sample)
    state_row = jax.ShapeDtypeStruct((1, n_state), F32)
    state_smp = jax.ShapeDtypeStruct((n_sample, n_state), F32)
    consts = (bre, bim, cre, cim, are, aim, akre, akim, d_skip, w_glu, b_glu, h0re, h0im)
    return pl.pallas_call(
        kernel,
        grid=(n_prompt_chunks + 1,),
        in_specs=[pl.BlockSpec((t, u.shape[1]), lambda c: (c, 0))] + [full(a) for a in consts],
        out_specs=[pl.BlockSpec((t, u.shape[1]), lambda c: (c, 0)),
                   full(state_row), full(state_row), full(state_smp), full(state_smp)],
        out_shape=[jax.ShapeDtypeStruct((rows, u.shape[1]), BF16),
                   state_row, state_row, state_smp, state_smp],
        scratch_shapes=[pltpu.VMEM((n_state // 128, t, 128), F32), pltpu.VMEM((n_state // 128, t, 128), F32),
                        pltpu.VMEM((8, n_state), F32), pltpu.VMEM((8, n_state), F32),
                        pltpu.VMEM((8, n_state), F32), pltpu.VMEM((8, n_state), F32),
                        pltpu.VMEM((1, n_state), F32), pltpu.VMEM((1, n_state), F32)],
        compiler_params=pltpu.CompilerParams(
            dimension_semantics=("arbitrary",), vmem_limit_bytes=VMEM_LIMIT),
        name="s5_branch",
    )(u, *consts)


def _merge_kernel(attn_ref, as_ref, ssm_ref, sga_ref, sgs_ref, xp_ref, xh_ref, meta_ref, xs_ref,
                  wba_ref, wbs_ref, wo_ref, g2_ref,
                  x1_ref, h2_ref, ma_scr, x_scr, *, sample_tile):
    n_sample = as_ref.shape[0]
    _assemble_rows(pl.program_id(0), sample_tile, xp_ref, xh_ref, meta_ref, xs_ref, x_scr)
    ma_scr[...] = _dot(attn_ref[...], wba_ref[...])

    @pl.when(pl.program_id(0) == sample_tile)
    def _():
        acc = jnp.zeros((n_sample, wba_ref.shape[1]), F32)
        for h in range(N_HEADS):
            acc = acc + _dot(as_ref[:, h, :].astype(BF16), wba_ref[h * D_HEAD:(h + 1) * D_HEAD, :])
        ma_scr[SAMPLE_OFF:SAMPLE_OFF + n_sample, :] = acc

    m = (sga_ref[...].astype(F32) * ma_scr[...]
         + sgs_ref[...].astype(F32) * _dot(ssm_ref[...], wbs_ref[...]))
    x1 = x_scr[...] + _dot(m.astype(BF16), wo_ref[...])
    x1_ref[...] = x1
    ms = jnp.mean(x1 * x1, axis=-1, keepdims=True)
    h2_ref[...] = ((x1 * lax.rsqrt(ms + EPS)) * g2_ref[...]).astype(BF16)


def _merge(attn, attn_s, ssm, sga, sgs, x_prompt, meta, x_sample, wba, wbs, wo, g2, *, tm, sample_tile):
    rows = attn.shape[0]
    n_meta, d = meta.shape
    row_blk = lambda w: pl.BlockSpec((tm, w), lambda i: (i, 0))
    const = lambda a: pl.BlockSpec(a.shape, lambda i: (0,) * a.ndim, pipeline_mode=pl.Buffered(1))
    kernel = functools.partial(_merge_kernel, sample_tile=sample_tile)
    return pl.pallas_call(
        kernel,
        grid=(rows // tm,),
        in_specs=[row_blk(D_ATTN), const(attn_s), row_blk(ssm.shape[1]), row_blk(d), row_blk(d)]
                 + _prompt_row_specs(tm, d, n_meta, sample_tile, 1)
                 + [const(meta), const(x_sample), const(wba), const(wbs), const(wo), const(g2)],
        out_specs=[row_blk(d), row_blk(d)],
        out_shape=[jax.ShapeDtypeStruct((rows, d), F32), jax.ShapeDtypeStruct((rows, d), BF16)],
        scratch_shapes=[pltpu.VMEM((tm, d), F32), pltpu.VMEM((tm, d), F32)],
        compiler_params=pltpu.CompilerParams(
            dimension_semantics=("arbitrary",), vmem_limit_bytes=VMEM_LIMIT),
        name="merge",
    )(attn, attn_s, ssm, sga, sgs, x_prompt, x_prompt, meta, x_sample, wba, wbs, wo, g2)


def _ffn_kernel(h2_ref, x1_ref, w1_ref, w3_ref, w2_ref, o_ref):
    @pl.when(pl.program_id(1) == 0)
    def _():
        o_ref[...] = x1_ref[...]

    h2 = h2_ref[...]
    a = _dot(h2, w1_ref[...])
    b = _dot(h2, w3_ref[...])
    o_ref[...] += _dot(((a * jax.nn.sigmoid(a)) * b).astype(BF16), w2_ref[...])


def _ffn(h2, x1, w1, w3, w2, *, tm, tf):
    rows, d = x1.shape
    d_ff = w1.shape[1]
    return pl.pallas_call(
        _ffn_kernel,
        grid=(rows // tm, d_ff // tf),
        in_specs=[pl.BlockSpec((tm, d), lambda i, f: (i, 0)),
                  pl.BlockSpec((tm, d), lambda i, f: (i, 0)),
                  pl.BlockSpec((d, tf), lambda i, f: (0, f)),
                  pl.BlockSpec((d, tf), lambda i, f: (0, f)),
                  pl.BlockSpec((tf, d), lambda i, f: (f, 0))],
        out_specs=pl.BlockSpec((tm, d), lambda i, f: (i, 0)),
        out_shape=jax.ShapeDtypeStruct((rows, d), F32),
        compiler_params=pltpu.CompilerParams(
            dimension_semantics=("arbitrary", "arbitrary"), vmem_limit_bytes=VMEM_LIMIT),
        name="ffn",
    )(h2, x1, w1, w3, w2)


def _ssm_tables(lam_re, lam_im, log_dt, b_re, b_im, c_re, c_im, steps):
    n_groups = lam_re.shape[0]
    lr, li = lam_re.astype(F32), lam_im.astype(F32)
    dt = jnp.exp(log_dt.astype(F32))[:, None]
    mag = jnp.exp(lr * dt)
    ar, ai = mag * jnp.cos(li * dt), mag * jnp.sin(li * dt)
    den = lr * lr + li * li
    cr = ((ar - 1.0) * lr + ai * li) / den
    ci = (ai * lr - (ar - 1.0) * li) / den
    br, bi = b_re.astype(F32), b_im.astype(F32)
    bbr = cr[..., None] * br - ci[..., None] * bi
    bbi = cr[..., None] * bi + ci[..., None] * br
    akr, aki = ar, ai
    for _ in range(steps - 1):
        akr, aki = akr * ar - aki * ai, akr * ai + aki * ar
    n_blk = n_groups // GROUPS_PER_BLOCK
    eye = jnp.eye(GROUPS_PER_BLOCK, dtype=F32)

    def in_blocks(m):
        m = m.reshape(n_blk, GROUPS_PER_BLOCK, P_STATE, SSM_GROUP)
        return jnp.einsum('jgpc,gh->jgchp', m, eye).reshape(
            n_blk, GROUPS_PER_BLOCK * SSM_GROUP, GROUPS_PER_BLOCK * P_STATE).astype(BF16)

    def out_blocks(m):
        m = m.reshape(n_blk, GROUPS_PER_BLOCK, SSM_GROUP, P_STATE)
        return jnp.einsum('jgcp,gh->jgphc', m, eye).reshape(
            n_blk, GROUPS_PER_BLOCK * P_STATE, GROUPS_PER_BLOCK * SSM_GROUP).astype(BF16)

    flat = lambda a: a.reshape(1, -1)
    return (in_blocks(bbr), in_blocks(bbi),
            out_blocks(c_re.astype(F32)), out_blocks(-c_im.astype(F32)),
            flat(ar), flat(ai), flat(akr), flat(aki))


def _tiles(l_prompt):
    if l_prompt == 8208:
        return dict(t_ssm=432, t_attn=640, rc=32, tm_ffn=864, tf=512, slabs_per_pass=4, pp=16, group=32)
    return dict(t_ssm=176, t_attn=128, rc=32, tm_ffn=352, tf=512, slabs_per_pass=4, pp=2, group=32)


def kernel(x_prompt, x_sample, cache_k, cache_v, cache_logf, state_ssm, page_table, meta, norm1_g, w_in, b_f, q_norm_g, k_norm_g, lam_re, lam_im, log_dt, b_re, b_im, c_re, c_im, d_skip, w_glu, b_glu, w_br_attn, w_br_ssm, w_out, norm2_g, w_ff1, w_ff3, w_ff2):
    assert x_prompt.shape[0] == 1 and x_sample.shape[1] == 1 and w_in.shape[0] == 1
    n_meta, d = meta.shape
    seq = x_prompt.shape[1]
    n_sample = x_sample.shape[0]
    l_prompt = n_meta + seq
    cfg = _tiles(l_prompt)
    t_ssm = cfg["t_ssm"]
    assert l_prompt % t_ssm == 0 and SAMPLE_OFF + n_sample <= t_ssm
    n_p = l_prompt // t_ssm
    rows = l_prompt + t_ssm
    s0 = l_prompt + SAMPLE_OFF
    d_ssm = w_glu.shape[1]
    n_groups = d_ssm // SSM_GROUP

    xp, xm, xs = x_prompt[0], meta.astype(F32), x_sample[:, 0]
    w = w_in[0]
    c0, c1 = 3 * D_ATTN, 3 * D_ATTN + N_HEADS
    n_w1 = w.shape[1] - N_HEADS
    col = lax.broadcasted_iota(jnp.int32, (1, n_w1), 1)
    w1 = jnp.where(col < c0, w[:, :n_w1], w[:, N_HEADS:]).astype(BF16)
    wf = jnp.pad(w[:, c0:c1], ((0, 0), (0, 128 - N_HEADS))).astype(BF16)
    bf = jnp.pad(b_f[0].astype(F32), (0, 128 - N_HEADS)).reshape(1, 128)
    row = lambda a: a.astype(F32).reshape(1, -1)

    q, kf, ks, kb, vf, vs, vb, u, sga, sgs, lf = _inproj(
        xp, xm, xs, row(norm1_g[0]), w1, wf, bf, row(q_norm_g[0]), row(k_norm_g[0]),
        tm=t_ssm, rows=rows)

    ta = cfg["t_attn"]
    nk = -(-l_prompt // ta)
    assert nk * ta <= rows
    c_rows = _cumsum_rows(lf, n_rows=nk * ta, t=ta)
    attn = _flash(q, kb, vb, c_rows.reshape(N_HEADS, nk, 1, ta), t=ta, rc=cfg["rc"])

    smp = slice(s0, s0 + n_sample)
    q_s = q[smp].astype(F32).reshape(n_sample, N_HEADS, D_HEAD)
    lf_cur = lf[smp].reshape(n_sample, 1, 128)
    attn_s = _decode(page_table, q_s, ks, vs, lf_cur, cache_k[0], cache_v[0],
                     jnp.swapaxes(cache_logf[0], 1, 2), pp=cfg["pp"], group=cfg["group"])

    steps = t_ssm // 8
    bre, bim, cre, cim, are, aim, akre, akim = _ssm_tables(
        lam_re[0], lam_im[0], log_dt[0], b_re[0], b_im[0], c_re[0], c_im[0], steps)
    h0 = state_ssm[0].astype(F32).reshape(n_sample, n_groups * P_STATE, 2)
    ssm, hpre, hpim, hsre, hsim = _ssm(
        u, bre, bim, cre, cim, are, aim, akre, akim,
        row(d_skip[0]), w_glu[0].astype(BF16), row(b_glu[0]), h0[..., 0], h0[..., 1],
        t=t_ssm, n_prompt_chunks=n_p, slabs_per_pass=cfg["slabs_per_pass"])

    x1, h2 = _merge(attn, attn_s, ssm, sga, sgs, xp, xm, xs, w_br_attn[0].astype(BF16),
                    w_br_ssm[0].astype(BF16), w_out[0].astype(BF16), row(norm2_g[0]),
                    tm=t_ssm, sample_tile=n_p)
    y = _ffn(h2, x1, w_ff1[0].astype(BF16), w_ff3[0].astype(BF16), w_ff2[0].astype(BF16),
             tm=cfg["tm_ffn"], tf=cfg["tf"])

    state = lambda re, im, n: jnp.stack([re, im], axis=-1).reshape(1, n, n_groups, P_STATE, 2)
    return (y[n_meta:l_prompt][None],
            y[smp][:, None],
            kf.reshape(1, 1, l_prompt, N_HEADS, D_HEAD),
            vf.reshape(1, 1, l_prompt, N_HEADS, D_HEAD),
            lf[:l_prompt, :N_HEADS].reshape(1, 1, l_prompt, N_HEADS),
            state(hpre, hpim, 1),
            ks.reshape(1, n_sample, 1, N_HEADS, D_HEAD),
            vs.reshape(1, n_sample, 1, N_HEADS, D_HEAD),
            lf[smp, :N_HEADS].reshape(1, n_sample, 1, N_HEADS),
            state(hsre, hsim, n_sample))
```

```python
import functools
import math

import jax
import jax.numpy as jnp
from jax import lax
from jax.experimental import pallas as pl
from jax.experimental.pallas import tpu as pltpu

F32 = jnp.float32
BF16 = jnp.bfloat16

N_HEADS = 8
D_HEAD = 128
D_ATTN = N_HEADS * D_HEAD
SSM_GROUP = 16
P_STATE = 64
EPS = 1e-6
SAMPLE_OFF = 16
GROUPS_PER_BLOCK = 8
VMEM_LIMIT = 56 * 1024 * 1024
LOG2E = math.log2(math.e)


def _dot(a, b):
    return jnp.dot(a, b, preferred_element_type=F32)


def _split3(x):
    hi = x.astype(BF16)
    r1 = x - hi.astype(F32)
    mid = r1.astype(BF16)
    lo = (r1 - mid.astype(F32)).astype(BF16)
    return hi, mid, lo


def _tree(op, xs):
    xs = list(xs)
    while len(xs) > 1:
        xs = [op(xs[i], xs[i + 1]) for i in range(0, len(xs) - 1, 2)] + ([xs[-1]] if len(xs) % 2 else [])
    return xs[0]


def _log_sigmoid(f):
    return jnp.minimum(f, 0.0) - jnp.log1p(jnp.exp(-jnp.abs(f)))


def _head_rmsnorm(z, g):
    outs = []
    for h in range(N_HEADS):
        zh = z[:, h * D_HEAD:(h + 1) * D_HEAD]
        ms = jnp.broadcast_to(jnp.mean(zh * zh, axis=-1, keepdims=True), zh.shape)
        outs.append(zh * lax.rsqrt(ms + EPS) * g)
    return jnp.concatenate(outs, axis=-1)


def _assemble_rows(i, n_prompt_tiles, xp_ref, xh_ref, meta_ref, xs_ref, x_scr):
    n_meta = meta_ref.shape[0]
    n_sample = xs_ref.shape[0]
    body = x_scr.shape[0] - n_meta

    @pl.when(i == 0)
    def _():
        x_scr[:n_meta, :] = meta_ref[...]
        x_scr[n_meta:, :] = xp_ref[:body, :]

    @pl.when(jnp.logical_and(i > 0, i < n_prompt_tiles))
    def _():
        x_scr[:n_meta, :] = xh_ref[...]
        x_scr[n_meta:, :] = xp_ref[:body, :]

    @pl.when(i == n_prompt_tiles)
    def _():
        x_scr[...] = jnp.zeros_like(x_scr)
        x_scr[SAMPLE_OFF:SAMPLE_OFF + n_sample, :] = xs_ref[...]


def _inproj_kernel(xp_ref, xh_ref, meta_ref, xs_ref, g1_ref, w_ref, wf_ref, bf_ref, qg_ref, kg_ref,
                   q_ref, kf_ref, ks_ref, kb_ref, vf_ref, vs_ref, vb_ref, u_ref, sga_ref, sgs_ref, lf_ref,
                   h_scr, x_scr, *, n_prompt_tiles, n_sample):
    i = pl.program_id(0)
    j = pl.program_id(1)
    smp = slice(SAMPLE_OFF, SAMPLE_OFF + n_sample)

    @pl.when(j == 0)
    def _():
        _assemble_rows(i, n_prompt_tiles, xp_ref, xh_ref, meta_ref, xs_ref, x_scr)
        x = x_scr[...]
        ms = jnp.mean(x * x, axis=-1, keepdims=True)
        hb = ((x * lax.rsqrt(ms + EPS)) * g1_ref[...]).astype(BF16)
        h_scr[...] = hb
        lf_ref[...] = _log_sigmoid(_dot(hb, wf_ref[...]) + bf_ref[...])

    def proj():
        return _dot(h_scr[...], w_ref[...])

    def store_heads(val, full_ref, smp_ref):
        @pl.when(i < n_prompt_tiles)
        def _():
            full_ref[...] = val

        @pl.when(i == n_prompt_tiles)
        def _():
            for h in range(N_HEADS):
                smp_ref[:, h, :] = val[smp, h * D_HEAD:(h + 1) * D_HEAD]

    @pl.when(j == 0)
    def _():
        q_ref[...] = (_head_rmsnorm(proj(), qg_ref[...]) * (D_HEAD ** -0.5 * LOG2E)).astype(BF16)

    @pl.when(j == 1)
    def _():
        kn = _head_rmsnorm(proj(), kg_ref[...])
        kb_ref[...] = kn.astype(BF16)
        store_heads(kn, kf_ref, ks_ref)

    @pl.when(j == 2)
    def _():
        z = proj()
        vb_ref[...] = z.astype(BF16)
        store_heads(z, vf_ref, vs_ref)

    @pl.when(j == 3)
    def _():
        u_ref[...] = proj()

    for jj, ref, half in ((4, sga_ref, 0), (5, sga_ref, 1), (6, sgs_ref, 0), (7, sgs_ref, 1)):
        @pl.when(j == jj)
        def _(ref=ref, half=half):
            ref[:, half * 1024:(half + 1) * 1024] = jax.nn.sigmoid(proj()).astype(BF16)


def _prompt_row_specs(tm, d, n_meta, n_prompt_tiles, grid_rank):
    del grid_rank
    assert tm % n_meta == 0
    per = tm // n_meta
    main = lambda i, *_: (jnp.minimum(i, n_prompt_tiles - 1), 0)
    head = lambda i, *_: (jnp.maximum(jnp.minimum(i, n_prompt_tiles - 1) * per - 1, 0), 0)
    return [pl.BlockSpec((tm, d), main), pl.BlockSpec((n_meta, d), head)]


def _inproj(x_prompt, meta, x_sample, g1, w1, wf, bf, qg, kg, *, tm, rows):
    n_meta, d = meta.shape
    n_sample = x_sample.shape[0]
    l_prompt = n_meta + x_prompt.shape[0]
    n_col = w1.shape[1] // 1024
    n_p = l_prompt // tm
    row_blk = lambda w: pl.BlockSpec((tm, w), lambda i, j: (i, 0))
    full = lambda a: pl.BlockSpec(a.shape, lambda i, j: (0,) * a.ndim)
    heads_p = pl.BlockSpec((tm, D_ATTN), lambda i, j: (jnp.minimum(i, n_p - 1), 0))
    heads_s = pl.BlockSpec((n_sample, N_HEADS, D_HEAD), lambda i, j: (0, 0, 0))
    f32_p = jax.ShapeDtypeStruct((l_prompt, D_ATTN), F32)
    f32_s = jax.ShapeDtypeStruct((n_sample, N_HEADS, D_HEAD), F32)
    outs = [
        (jax.ShapeDtypeStruct((rows, D_ATTN), BF16), row_blk(D_ATTN)),
        (f32_p, heads_p), (f32_s, heads_s),
        (jax.ShapeDtypeStruct((rows, D_ATTN), BF16), row_blk(D_ATTN)),
        (f32_p, heads_p), (f32_s, heads_s),
        (jax.ShapeDtypeStruct((rows, D_ATTN), BF16), row_blk(D_ATTN)),
        (jax.ShapeDtypeStruct((rows, 1024), F32), row_blk(1024)),
        (jax.ShapeDtypeStruct((rows, d), BF16), row_blk(d)),
        (jax.ShapeDtypeStruct((rows, d), BF16), row_blk(d)),
        (jax.ShapeDtypeStruct((rows, 128), F32), row_blk(128)),
    ]
    kernel = functools.partial(_inproj_kernel, n_prompt_tiles=n_p, n_sample=n_sample)
    return pl.pallas_call(
        kernel,
        grid=(rows // tm, n_col),
        in_specs=_prompt_row_specs(tm, d, n_meta, n_p, 2) + [
                  full(meta), full(x_sample), full(g1), pl.BlockSpec((d, 1024), lambda i, j: (0, j)),
                  full(wf), full(bf), full(qg), full(kg)],
        out_specs=[o[1] for o in outs],
        out_shape=[o[0] for o in outs],
        scratch_shapes=[pltpu.VMEM((tm, d), BF16), pltpu.VMEM((tm, d), F32)],
        compiler_params=pltpu.CompilerParams(
            dimension_semantics=("arbitrary", "arbitrary"), vmem_limit_bytes=VMEM_LIMIT),
        name="inproj",
    )(x_prompt, x_prompt, meta, x_sample, g1, w1, wf, bf, qg, kg)


def _cumsum_kernel(lf_ref, crow_ref, carry_scr):
    @pl.when(pl.program_id(0) == 0)
    def _():
        carry_scr[...] = jnp.zeros_like(carry_scr)

    lf = lf_ref[...]
    t = lf.shape[0]
    r = lax.broadcasted_iota(jnp.int32, (t, t), 0)
    c = lax.broadcasted_iota(jnp.int32, (t, t), 1)
    tri = (c <= r).astype(BF16)
    hi, mid, lo = _split3(lf)
    cs = _dot(tri, hi) + _dot(tri, mid) + _dot(tri, lo) + carry_scr[...]
    carry_scr[...] = cs[t - 1:t, :]
    crow_ref[...] = (cs * LOG2E).T[:N_HEADS, :]


def _cumsum_rows(lf, *, n_rows, t):
    return pl.pallas_call(
        _cumsum_kernel,
        grid=(n_rows // t,),
        in_specs=[pl.BlockSpec((t, 128), lambda i: (i, 0))],
        out_specs=pl.BlockSpec((N_HEADS, t), lambda i: (0, i)),
        out_shape=jax.ShapeDtypeStruct((N_HEADS, n_rows), F32),
        scratch_shapes=[pltpu.VMEM((1, 128), F32)],
        compiler_params=pltpu.CompilerParams(dimension_semantics=("arbitrary",)),
        name="logf_cumsum",
    )(lf)


def _flash_kernel(q_ref, k_ref, v_ref, c_ref, zero_ref, o_ref,
                  s_scr, p_scr, m_scr, l_scr, a_scr, acc_scr, *, t, rc):
    del zero_ref
    i = pl.program_id(1)
    q = q_ref[...]
    m_scr[...] = jnp.full_like(m_scr, -jnp.inf)
    l_scr[...] = jnp.zeros_like(l_scr)
    acc_scr[...] = jnp.zeros_like(acc_scr)

    def scores(j, slot):
        k0 = pl.multiple_of(j * t, t)
        s_scr[slot] = lax.dot_general(q, k_ref[pl.ds(k0, t), :], (((1,), (1,)), ((), ())),
                                      preferred_element_type=F32)

    def weighted_values(j, slot):
        k0 = pl.multiple_of(j * t, t)
        acc_scr[...] = a_scr[slot] * acc_scr[...] + _dot(p_scr[slot], v_ref[pl.ds(k0, t), :])

    def softmax(j, slot, masked):
        cj = c_ref[j]

        def chunk(r, carry):
            rows = pl.ds(pl.multiple_of(r * rc, rc), rc)
            s = s_scr[slot, rows, :] - cj
            if masked:
                row = r * rc + lax.broadcasted_iota(jnp.int32, (rc, t), 0)
                col = lax.broadcasted_iota(jnp.int32, (rc, t), 1)
                s = jnp.where(col <= row, s, -jnp.inf)
            m_prev = m_scr[rows, :]
            m_new = jnp.maximum(m_prev, jnp.broadcast_to(jnp.max(s, axis=-1, keepdims=True), (rc, D_HEAD)))
            alpha = jnp.exp2(m_prev - m_new)
            p = jnp.exp2(s - jnp.tile(m_new, (1, t // D_HEAD)))
            l_scr[rows, :] = alpha * l_scr[rows, :] + jnp.broadcast_to(
                jnp.sum(p, axis=-1, keepdims=True), (rc, D_HEAD))
            m_scr[rows, :] = m_new
            a_scr[slot, rows, :] = alpha
            p_scr[slot, rows, :] = p.astype(BF16)
            return carry

        lax.fori_loop(0, t // rc, chunk, 0, unroll=True)

    p_scr[1] = jnp.zeros((t, t), BF16)
    a_scr[1] = jnp.ones((t, D_HEAD), F32)
    scores(0, 0)

    def pair(jj, carry):
        j = 2 * jj

        @pl.when(jj >= 0)
        def _():
            scores(j + 1, 1)
            softmax(j, 0, False)
            weighted_values(jnp.maximum(j - 1, 0), 1)

        @pl.when(jj >= 0)
        def _():
            scores(j + 2, 0)
            softmax(j + 1, 1, False)
            weighted_values(j, 0)

        return carry

    n_pairs = i // 2
    lax.fori_loop(0, n_pairs, pair, 0)
    prev = jnp.maximum(2 * n_pairs - 1, 0)

    @pl.when(i % 2 == 0)
    def _():
        softmax(i, 0, True)
        weighted_values(prev, 1)
        weighted_values(i, 0)

    @pl.when(i % 2 == 1)
    def _():
        scores(i, 1)
        softmax(i - 1, 0, False)
        weighted_values(prev, 1)

    @pl.when(i % 2 == 1)
    def _():
        softmax(i, 1, True)
        weighted_values(i - 1, 0)
        weighted_values(i, 1)

    o_ref[...] = (acc_scr[...] * (1.0 / l_scr[...])).astype(o_ref.dtype)


def _flash(q, kb, vb, c_tiles, *, t, rc):
    rows = q.shape[0]
    nk = c_tiles.shape[1]
    kernel = functools.partial(_flash_kernel, t=t, rc=rc)
    return pl.pallas_call(
        kernel,
        grid=(N_HEADS, nk),
        in_specs=[pl.BlockSpec((t, D_HEAD), lambda h, i: (i, h)),
                  pl.BlockSpec((nk * t, D_HEAD), lambda h, i: (0, h)),
                  pl.BlockSpec((nk * t, D_HEAD), lambda h, i: (0, h)),
                  pl.BlockSpec((None, nk, 1, t), lambda h, i: (h, 0, 0, 0)),
                  pl.BlockSpec(memory_space=pl.ANY)],
        out_specs=pl.BlockSpec((t, D_HEAD), lambda h, i: (i, h)),
        out_shape=jax.ShapeDtypeStruct((rows, D_ATTN), BF16),
        input_output_aliases={4: 0},
        scratch_shapes=[pltpu.VMEM((2, t, t), F32), pltpu.VMEM((2, t, t), BF16),
                        pltpu.VMEM((t, D_HEAD), F32), pltpu.VMEM((t, D_HEAD), F32),
                        pltpu.VMEM((2, t, D_HEAD), F32), pltpu.VMEM((t, D_HEAD), F32)],
        compiler_params=pltpu.CompilerParams(
            dimension_semantics=("arbitrary", "arbitrary"), vmem_limit_bytes=VMEM_LIMIT),
        name="fox_prompt",
    )(q, kb, vb, c_tiles, jnp.zeros((rows, D_ATTN), BF16))


def _decode_kernel(pt_ref, q_ref, kc_ref, vc_ref, lfc_ref, *refs, pp, group):
    del pt_ref
    k_refs = refs[:pp]
    v_refs = refs[pp:2 * pp]
    lf_refs = refs[2 * pp:3 * pp]
    o_ref = refs[3 * pp]
    m_scr, l_scr, acc_scr, carry_scr, wide_scr = refs[3 * pp + 1:]
    g = pl.program_id(1)
    n_g = pl.num_programs(1)
    page = lf_refs[0].shape[-1]

    q = q_ref[0]
    lane = lax.broadcasted_iota(jnp.int32, (N_HEADS, 128), 1)
    sub = lax.broadcasted_iota(jnp.int32, (N_HEADS, 128), 0)
    diag = lane == sub
    later = (lax.broadcasted_iota(jnp.int32, (page, page), 0)
             > lax.broadcasted_iota(jnp.int32, (page, page), 1)).astype(F32)

    @pl.when(g == 0)
    def _():
        m_scr[...] = jnp.full_like(m_scr, -jnp.inf)
        l_scr[...] = jnp.zeros_like(l_scr)
        acc_scr[...] = jnp.zeros_like(acc_scr)
        cur = jnp.where(diag, jnp.broadcast_to(lfc_ref[0], (N_HEADS, 128)), 0.0)
        carry_scr[...] = jnp.broadcast_to(jnp.sum(cur, axis=-1, keepdims=True) * LOG2E, (N_HEADS, 128))

    m_run = m_scr[...]
    l_run = l_scr[...]
    acc_run = acc_scr[...]
    for pi in range(pp):
        x = lf_refs[pi][...] * LOG2E
        hi, mid, lo = _split3(x)
        carry = carry_scr[...]
        bias = (_dot(hi.astype(F32), later) + _dot(mid.astype(F32), later)
                + _dot(lo.astype(F32), later) + carry)
        carry_scr[...] = carry + jnp.sum(x, axis=-1, keepdims=True)
        wide_scr[pi] = jnp.concatenate([bias, jnp.zeros((page - N_HEADS, page), F32)], axis=0).T
        k_ref, v_ref = k_refs[pi], v_refs[pi]
        for t0 in range(0, page, group):
            ss = []
            for t in range(t0, t0 + group):
                b_t = jnp.where(diag, jnp.broadcast_to(wide_scr[pi, t:t + 1, :], (N_HEADS, 128)), 0.0)
                s_t = jnp.sum(k_ref[t] * q + b_t, axis=-1, keepdims=True)
                ss.append(jnp.broadcast_to(s_t, (N_HEADS, 128)))
            m_new = jnp.maximum(m_run, _tree(jnp.maximum, ss))
            alpha = jnp.exp2(m_run - m_new)
            ps = [jnp.exp2(s_t - m_new) for s_t in ss]
            l_run = alpha * l_run + _tree(jnp.add, ps)
            acc_run = alpha * acc_run + _tree(jnp.add, [p_t * v_ref[t0 + n] for n, p_t in enumerate(ps)])
            m_run = m_new

    m_scr[...] = m_run
    l_scr[...] = l_run
    acc_scr[...] = acc_run

    @pl.when(g == n_g - 1)
    def _():
        s_cur = jnp.sum(kc_ref[0] * q, axis=-1, keepdims=True)
        m_new = jnp.maximum(m_run, s_cur)
        alpha = jnp.exp2(m_run - m_new)
        p = jnp.exp2(s_cur - m_new)
        l = alpha * l_run + p
        acc = alpha * acc_run + p * vc_ref[0]
        o_ref[0] = acc * (1.0 / l)


def _decode(page_table, q_s, k_cur, v_cur, lf_cur, cache_k, cache_v, cache_lft, *, pp, group):
    b, n_pages = page_table.shape
    page = cache_lft.shape[-1]
    n_g = n_pages // pp
    kernel = functools.partial(_decode_kernel, pp=pp, group=group)
    seq_blk = lambda w: pl.BlockSpec((1, w, 128), lambda s, g, pt: (s, 0, 0))

    def page_map(pi, nd):
        def index_map(s, g, pt):
            return (pt[s * n_pages + (n_pages - 1 - (g * pp + pi))],) + (0,) * nd
        return index_map

    k_specs = [pl.BlockSpec((None, page, N_HEADS, D_HEAD), page_map(pi, 3)) for pi in range(pp)]
    v_specs = [pl.BlockSpec((None, page, N_HEADS, D_HEAD), page_map(pi, 3)) for pi in range(pp)]
    lf_specs = [pl.BlockSpec((None, N_HEADS, page), page_map(pi, 2)) for pi in range(pp)]
    grid_spec = pltpu.PrefetchScalarGridSpec(
        num_scalar_prefetch=1,
        grid=(b, n_g),
        in_specs=[seq_blk(N_HEADS), seq_blk(N_HEADS), seq_blk(N_HEADS), seq_blk(1)]
                 + k_specs + v_specs + lf_specs,
        out_specs=pl.BlockSpec((1, N_HEADS, D_HEAD), lambda s, g, pt: (s, 0, 0)),
        scratch_shapes=[pltpu.VMEM((N_HEADS, 128), F32), pltpu.VMEM((N_HEADS, 128), F32),
                        pltpu.VMEM((N_HEADS, D_HEAD), F32), pltpu.VMEM((N_HEADS, 128), F32),
                        pltpu.VMEM((pp, page, page), F32)],
    )
    return pl.pallas_call(
        kernel,
        grid_spec=grid_spec,
        out_shape=jax.ShapeDtypeStruct((b, N_HEADS, D_HEAD), F32),
        compiler_params=pltpu.CompilerParams(
            dimension_semantics=("arbitrary", "arbitrary"), vmem_limit_bytes=VMEM_LIMIT),
        name="fox_decode",
    )(page_table.reshape(-1), q_s, k_cur, v_cur, lf_cur,
      *([cache_k] * pp), *([cache_v] * pp), *([cache_lft] * pp))


def _gelu_tanh(y):
    return 0.5 * y * (1.0 + jnp.tanh(math.sqrt(2.0 / math.pi) * (y + 0.044715 * (y * y * y))))


def _ssm_kernel(u_ref, bre_ref, bim_ref, cre_ref, cim_ref,
                are_ref, aim_ref, akre_ref, akim_ref, d_ref, wg_ref, bg_ref, h0re_ref, h0im_ref,
                o_ref, hpre_ref, hpim_ref, hsre_ref, hsim_ref,
                sre_scr, sim_scr, endre_scr, endim_scr, hinre_scr, hinim_scr, cre_scr, cim_scr,
                *, n_prompt_chunks, steps, slabs_per_pass, n_sample):
    c = pl.program_id(0)
    n_slab = sre_scr.shape[0]
    n_blk = bre_ref.shape[0]
    blk_slabs = n_slab // n_blk
    lanes = lambda sb: slice(sb * 128, (sb + 1) * 128)

    def input_proj(ub):
        for j in range(n_blk):
            uj = ub[:, j * 128:(j + 1) * 128]
            zr = _dot(uj, bre_ref[j])
            zi = _dot(uj, bim_ref[j])
            for i in range(blk_slabs):
                sre_scr[j * blk_slabs + i] = zr[:, lanes(i)]
                sim_scr[j * blk_slabs + i] = zi[:, lanes(i)]

    def tail(uf):
        ys = []
        for j in range(n_blk):
            sbs = range(j * blk_slabs, (j + 1) * blk_slabs)
            hr = jnp.concatenate([sre_scr[sb] for sb in sbs], axis=-1).astype(BF16)
            hi = jnp.concatenate([sim_scr[sb] for sb in sbs], axis=-1).astype(BF16)
            ys.append(_dot(hr, cre_ref[j]) + _dot(hi, cim_ref[j]))
        y = jnp.concatenate(ys, axis=-1) + d_ref[...] * uf
        gl = _gelu_tanh(y)
        return gl * jax.nn.sigmoid(_dot(gl.astype(BF16), wg_ref[...]) + bg_ref[...])

    def scan_pass(store):
        for g0 in range(0, n_slab, slabs_per_pass):
            sbs = list(range(g0, g0 + slabs_per_pass))
            ar = [jnp.broadcast_to(are_ref[:, lanes(sb)], (8, 128)) for sb in sbs]
            ai = [jnp.broadcast_to(aim_ref[:, lanes(sb)], (8, 128)) for sb in sbs]
            if store:
                init = ([hinre_scr[:, lanes(sb)] for sb in sbs], [hinim_scr[:, lanes(sb)] for sb in sbs])
            else:
                zero = jnp.zeros((8, 128), F32)
                init = ([zero] * len(sbs), [zero] * len(sbs))

            def body(k, carry, sbs=sbs, ar=ar, ai=ai):
                re, im = carry
                rows = pl.ds(k, 8, stride=steps)
                nre, nim = [], []
                for n, sb in enumerate(sbs):
                    r = ar[n] * re[n] - ai[n] * im[n] + sre_scr[sb, rows, :]
                    i = ar[n] * im[n] + ai[n] * re[n] + sim_scr[sb, rows, :]
                    if store:
                        sre_scr[sb, rows, :] = r
                        sim_scr[sb, rows, :] = i
                    nre.append(r)
                    nim.append(i)
                return nre, nim

            re, im = lax.fori_loop(0, steps, body, init, unroll=3)
            if not store:
                for n, sb in enumerate(sbs):
                    endre_scr[:, lanes(sb)] = re[n]
                    endim_scr[:, lanes(sb)] = im[n]

    @pl.when(c == 0)
    def _():
        cre_scr[...] = jnp.zeros_like(cre_scr)
        cim_scr[...] = jnp.zeros_like(cim_scr)

    @pl.when(c < n_prompt_chunks)
    def _():
        uf = u_ref[...]
        input_proj(uf.astype(BF16))
        scan_pass(False)
        cr = cre_scr[...]
        ci = cim_scr[...]
        akr = akre_ref[...]
        aki = akim_ref[...]
        for s in range(8):
            hinre_scr[s:s + 1, :] = cr
            hinim_scr[s:s + 1, :] = ci
            er = endre_scr[s:s + 1, :]
            ei = endim_scr[s:s + 1, :]
            cr, ci = er + (akr * cr - aki * ci), ei + (akr * ci + aki * cr)
        cre_scr[...] = cr
        cim_scr[...] = ci
        hpre_ref[...] = cr
        hpim_ref[...] = ci
        scan_pass(True)
        o_ref[...] = tail(uf).astype(o_ref.dtype)

    @pl.when(c == n_prompt_chunks)
    def _():
        uf = u_ref[...]
        input_proj(uf.astype(BF16))
        rs = slice(SAMPLE_OFF, SAMPLE_OFF + n_sample)
        for sb in range(n_slab):
            ar = are_ref[:, lanes(sb)]
            ai = aim_ref[:, lanes(sb)]
            h0r = h0re_ref[:, lanes(sb)]
            h0i = h0im_ref[:, lanes(sb)]
            nre = ar * h0r - ai * h0i + sre_scr[sb, rs, :]
            nim = ar * h0i + ai * h0r + sim_scr[sb, rs, :]
            sre_scr[sb, rs, :] = nre
            sim_scr[sb, rs, :] = nim
            hsre_ref[:, lanes(sb)] = nre
            hsim_ref[:, lanes(sb)] = nim
        o_ref[...] = tail(uf).astype(o_ref.dtype)


def _ssm(u, bre, bim, cre, cim, are, aim, akre, akim, d_skip, w_glu, b_glu, h0re, h0im,
         *, t, n_prompt_chunks, slabs_per_pass):
    rows = u.shape[0]
    n_state = are.shape[1]
    n_sample = h0re.shape[0]
    full = lambda a: pl.BlockSpec(a.shape, lambda c: (0,) * a.ndim)
    kernel = functools.partial(_ssm_kernel, n_prompt_chunks=n_prompt_chunks, steps=t // 8,
                               slabs_per_pass=slabs_per_pass, n_sample=n_sample)
    state_row = jax.ShapeDtypeStruct((1, n_state), F32)
    state_smp = jax.ShapeDtypeStruct((n_sample, n_state), F32)
    consts = (bre, bim, cre, cim, are, aim, akre, akim, d_skip, w_glu, b_glu, h0re, h0im)
    return pl.pallas_call(
        kernel,
        grid=(n_prompt_chunks + 1,),
        in_specs=[pl.BlockSpec((t, u.shape[1]), lambda c: (c, 0))] + [full(a) for a in consts],
        out_specs=[pl.BlockSpec((t, u.shape[1]), lambda c: (c, 0)),
                   full(state_row), full(state_row), full(state_smp), full(state_smp)],
        out_shape=[jax.ShapeDtypeStruct((rows, u.shape[1]), BF16),
                   state_row, state_row, state_smp, state_smp],
        scratch_shapes=[pltpu.VMEM((n_state // 128, t, 128), F32), pltpu.VMEM((n_state // 128, t, 128), F32),
                        pltpu.VMEM((8, n_state), F32), pltpu.VMEM((8, n_state), F32),
                        pltpu.VMEM((8, n_state), F32), pltpu.VMEM((8, n_state), F32),
                        pltpu.VMEM((1, n_state), F32), pltpu.VMEM((1, n_state), F32)],
        compiler_params=pltpu.CompilerParams(
            dimension_semantics=("arbitrary",), vmem_limit_bytes=VMEM_LIMIT),
        name="s5_branch",
    )(u, *consts)


def _merge_kernel(attn_ref, as_ref, ssm_ref, sga_ref, sgs_ref, xp_ref, xh_ref, meta_ref, xs_ref,
                  wba_ref, wbs_ref, wo_ref, g2_ref,
                  x1_ref, h2_ref, ma_scr, x_scr, *, sample_tile):
    n_sample = as_ref.shape[0]
    _assemble_rows(pl.program_id(0), sample_tile, xp_ref, xh_ref, meta_ref, xs_ref, x_scr)
    ma_scr[...] = _dot(attn_ref[...], wba_ref[...])

    @pl.when(pl.program_id(0) == sample_tile)
    def _():
        acc = jnp.zeros((n_sample, wba_ref.shape[1]), F32)
        for h in range(N_HEADS):
            acc = acc + _dot(as_ref[:, h, :].astype(BF16), wba_ref[h * D_HEAD:(h + 1) * D_HEAD, :])
        ma_scr[SAMPLE_OFF:SAMPLE_OFF + n_sample, :] = acc

    m = (sga_ref[...].astype(F32) * ma_scr[...]
         + sgs_ref[...].astype(F32) * _dot(ssm_ref[...], wbs_ref[...]))
    x1 = x_scr[...] + _dot(m.astype(BF16), wo_ref[...])
    x1_ref[...] = x1
    ms = jnp.mean(x1 * x1, axis=-1, keepdims=True)
    h2_ref[...] = ((x1 * lax.rsqrt(ms + EPS)) * g2_ref[...]).astype(BF16)


def _merge(attn, attn_s, ssm, sga, sgs, x_prompt, meta, x_sample, wba, wbs, wo, g2, *, tm, sample_tile):
    rows = attn.shape[0]
    n_meta, d = meta.shape
    row_blk = lambda w: pl.BlockSpec((tm, w), lambda i: (i, 0))
    const = lambda a: pl.BlockSpec(a.shape, lambda i: (0,) * a.ndim, pipeline_mode=pl.Buffered(1))
    kernel = functools.partial(_merge_kernel, sample_tile=sample_tile)
    return pl.pallas_call(
        kernel,
        grid=(rows // tm,),
        in_specs=[row_blk(D_ATTN), const(attn_s), row_blk(ssm.shape[1]), row_blk(d), row_blk(d)]
                 + _prompt_row_specs(tm, d, n_meta, sample_tile, 1)
                 + [const(meta), const(x_sample), const(wba), const(wbs), const(wo), const(g2)],
        out_specs=[row_blk(d), row_blk(d)],
        out_shape=[jax.ShapeDtypeStruct((rows, d), F32), jax.ShapeDtypeStruct((rows, d), BF16)],
        scratch_shapes=[pltpu.VMEM((tm, d), F32), pltpu.VMEM((tm, d), F32)],
        compiler_params=pltpu.CompilerParams(
            dimension_semantics=("arbitrary",), vmem_limit_bytes=VMEM_LIMIT),
        name="merge",
    )(attn, attn_s, ssm, sga, sgs, x_prompt, x_prompt, meta, x_sample, wba, wbs, wo, g2)


def _ffn_kernel(h2_ref, x1_ref, w1_ref, w3_ref, w2_ref, o_ref):
    @pl.when(pl.program_id(1) == 0)
    def _():
        o_ref[...] = x1_ref[...]

    h2 = h2_ref[...]
    a = _dot(h2, w1_ref[...])
    b = _dot(h2, w3_ref[...])
    o_ref[...] += _dot(((a * jax.nn.sigmoid(a)) * b).astype(BF16), w2_ref[...])


def _ffn(h2, x1, w1, w3, w2, *, tm, tf):
    rows, d = x1.shape
    d_ff = w1.shape[1]
    return pl.pallas_call(
        _ffn_kernel,
        grid=(rows // tm, d_ff // tf),
        in_specs=[pl.BlockSpec((tm, d), lambda i, f: (i, 0)),
                  pl.BlockSpec((tm, d), lambda i, f: (i, 0), pipeline_mode=pl.Buffered(1)),
                  pl.BlockSpec((d, tf), lambda i, f: (0, f)),
                  pl.BlockSpec((d, tf), lambda i, f: (0, f)),
                  pl.BlockSpec((tf, d), lambda i, f: (f, 0))],
        out_specs=pl.BlockSpec((tm, d), lambda i, f: (i, 0)),
        out_shape=jax.ShapeDtypeStruct((rows, d), F32),
        compiler_params=pltpu.CompilerParams(
            dimension_semantics=("arbitrary", "arbitrary"), vmem_limit_bytes=VMEM_LIMIT),
        name="ffn",
    )(h2, x1, w1, w3, w2)


def _ssm_tables(lam_re, lam_im, log_dt, b_re, b_im, c_re, c_im, steps):
    n_groups = lam_re.shape[0]
    lr, li = lam_re.astype(F32), lam_im.astype(F32)
    dt = jnp.exp(log_dt.astype(F32))[:, None]
    mag = jnp.exp(lr * dt)
    ar, ai = mag * jnp.cos(li * dt), mag * jnp.sin(li * dt)
    den = lr * lr + li * li
    cr = ((ar - 1.0) * lr + ai * li) / den
    ci = (ai * lr - (ar - 1.0) * li) / den
    br, bi = b_re.astype(F32), b_im.astype(F32)
    bbr = cr[..., None] * br - ci[..., None] * bi
    bbi = cr[..., None] * bi + ci[..., None] * br
    akr, aki = ar, ai
    for _ in range(steps - 1):
        akr, aki = akr * ar - aki * ai, akr * ai + aki * ar
    n_blk = n_groups // GROUPS_PER_BLOCK
    eye = jnp.eye(GROUPS_PER_BLOCK, dtype=F32)

    def in_blocks(m):
        m = m.reshape(n_blk, GROUPS_PER_BLOCK, P_STATE, SSM_GROUP)
        return jnp.einsum('jgpc,gh->jgchp', m, eye).reshape(
            n_blk, GROUPS_PER_BLOCK * SSM_GROUP, GROUPS_PER_BLOCK * P_STATE).astype(BF16)

    def out_blocks(m):
        m = m.reshape(n_blk, GROUPS_PER_BLOCK, SSM_GROUP, P_STATE)
        return jnp.einsum('jgcp,gh->jgphc', m, eye).reshape(
            n_blk, GROUPS_PER_BLOCK * P_STATE, GROUPS_PER_BLOCK * SSM_GROUP).astype(BF16)

    flat = lambda a: a.reshape(1, -1)
    return (in_blocks(bbr), in_blocks(bbi),
            out_blocks(c_re.astype(F32)), out_blocks(-c_im.astype(F32)),
            flat(ar), flat(ai), flat(akr), flat(aki))


def _tiles(l_prompt):
    if l_prompt == 8208:
        return dict(t_ssm=432, t_attn=640, rc=32, tm_ffn=960, tf=512, slabs_per_pass=4, pp=16, group=32)
    return dict(t_ssm=176, t_attn=128, rc=32, tm_ffn=352, tf=512, slabs_per_pass=4, pp=2, group=32)


def kernel(x_prompt, x_sample, cache_k, cache_v, cache_logf, state_ssm, page_table, meta, norm1_g, w_in, b_f, q_norm_g, k_norm_g, lam_re, lam_im, log_dt, b_re, b_im, c_re, c_im, d_skip, w_glu, b_glu, w_br_attn, w_br_ssm, w_out, norm2_g, w_ff1, w_ff3, w_ff2):
    assert x_prompt.shape[0] == 1 and x_sample.shape[1] == 1 and w_in.shape[0] == 1
    n_meta, d = meta.shape
    seq = x_prompt.shape[1]
    n_sample = x_sample.shape[0]
    l_prompt = n_meta + seq
    cfg = _tiles(l_prompt)
    t_ssm = cfg["t_ssm"]
    assert l_prompt % t_ssm == 0 and SAMPLE_OFF + n_sample <= t_ssm
    n_p = l_prompt // t_ssm
    rows = l_prompt + t_ssm
    s0 = l_prompt + SAMPLE_OFF
    d_ssm = w_glu.shape[1]
    n_groups = d_ssm // SSM_GROUP

    xp, xm, xs = x_prompt[0], meta.astype(F32), x_sample[:, 0]
    w = w_in[0]
    c0, c1 = 3 * D_ATTN, 3 * D_ATTN + N_HEADS
    n_w1 = w.shape[1] - N_HEADS
    col = lax.broadcasted_iota(jnp.int32, (1, n_w1), 1)
    w1 = jnp.where(col < c0, w[:, :n_w1], w[:, N_HEADS:]).astype(BF16)
    wf = jnp.pad(w[:, c0:c1], ((0, 0), (0, 128 - N_HEADS))).astype(BF16)
    bf = jnp.pad(b_f[0].astype(F32), (0, 128 - N_HEADS)).reshape(1, 128)
    row = lambda a: a.astype(F32).reshape(1, -1)

    q, kf, ks, kb, vf, vs, vb, u, sga, sgs, lf = _inproj(
        xp, xm, xs, row(norm1_g[0]), w1, wf, bf, row(q_norm_g[0]), row(k_norm_g[0]),
        tm=t_ssm, rows=rows)

    ta = cfg["t_attn"]
    nk = -(-l_prompt // ta)
    assert nk * ta <= rows
    c_rows = _cumsum_rows(lf, n_rows=nk * ta, t=ta)
    attn = _flash(q, kb, vb, c_rows.reshape(N_HEADS, nk, 1, ta), t=ta, rc=cfg["rc"])

    smp = slice(s0, s0 + n_sample)
    q_s = q[smp].astype(F32).reshape(n_sample, N_HEADS, D_HEAD)
    lf_cur = lf[smp].reshape(n_sample, 1, 128)
    attn_s = _decode(page_table, q_s, ks, vs, lf_cur, cache_k[0], cache_v[0],
                     jnp.swapaxes(cache_logf[0], 1, 2), pp=cfg["pp"], group=cfg["group"])

    steps = t_ssm // 8
    bre, bim, cre, cim, are, aim, akre, akim = _ssm_tables(
        lam_re[0], lam_im[0], log_dt[0], b_re[0], b_im[0], c_re[0], c_im[0], steps)
    h0 = state_ssm[0].astype(F32).reshape(n_sample, n_groups * P_STATE, 2)
    ssm, hpre, hpim, hsre, hsim = _ssm(
        u, bre, bim, cre, cim, are, aim, akre, akim,
        row(d_skip[0]), w_glu[0].astype(BF16), row(b_glu[0]), h0[..., 0], h0[..., 1],
        t=t_ssm, n_prompt_chunks=n_p, slabs_per_pass=cfg["slabs_per_pass"])

    x1, h2 = _merge(attn, attn_s, ssm, sga, sgs, xp, xm, xs, w_br_attn[0].astype(BF16),
                    w_br_ssm[0].astype(BF16), w_out[0].astype(BF16), row(norm2_g[0]),
                    tm=t_ssm, sample_tile=n_p)
    y = _ffn(h2, x1, w_ff1[0].astype(BF16), w_ff3[0].astype(BF16), w_ff2[0].astype(BF16),
             tm=cfg["tm_ffn"], tf=cfg["tf"])

    state = lambda re, im, n: jnp.stack([re, im], axis=-1).reshape(1, n, n_groups, P_STATE, 2)
    return (y[n_meta:l_prompt][None],
            y[smp][:, None],
            kf.reshape(1, 1, l_prompt, N_HEADS, D_HEAD),
            vf.reshape(1, 1, l_prompt, N_HEADS, D_HEAD),
            lf[:l_prompt, :N_HEADS].reshape(1, 1, l_prompt, N_HEADS),
            state(hpre, hpim, 1),
            ks.reshape(1, n_sample, 1, N_HEADS, D_HEAD),
            vs.reshape(1, n_sample, 1, N_HEADS, D_HEAD),
            lf[smp, :N_HEADS].reshape(1, n_sample, 1, N_HEADS),
            state(hsre, hsim, n_sample))
```
